```python
import math
import jax
import jax.numpy as jnp
from jax import lax
import numpy as np

D_MODEL = 2048
BATCH = 2
SEQ = 16384
DEPTH = 2

MEM_LEN = 256
HALF_MIX = D_MODEL // 2
MIX_WIDTH = 2 * HALF_MIX
IN_WIDTH = 5 * HALF_MIX
DIFF_HEADS = 4
DIFF_HEAD_DIM = HALF_MIX // (2 * DIFF_HEADS)
DIFF_V_DIM = 2 * DIFF_HEAD_DIM
ROPE_DIM = DIFF_HEAD_DIM // 4
ROPE_THETA = 500000.0
Q_BLOCK = 128
CONF_WIDTH = HALF_MIX
CONF_KERNEL = 31
SGU_WIDTH = HALF_MIX
SGU_GROUPS = 4
SGU_GROUP_DIM = SGU_WIDTH // SGU_GROUPS
SGU_CHUNK = 128
SC_WIDTH = HALF_MIX
SC_KERNEL = 3
CROSS_HEADS = 4
CROSS_HEAD_DIM = 128
CROSS_WIDTH = CROSS_HEADS * CROSS_HEAD_DIM
N_GROUPS = 4
EXPERTS_PER_GROUP = 8
N_EXPERTS = N_GROUPS * EXPERTS_PER_GROUP
TOP_K = 2
EXPERT_FF = D_MODEL // 4
MOE_BLOCK = 128
N_EVEN = (DEPTH + 1) // 2
N_ODD = DEPTH // 2
ALPHA = (2 * DEPTH) ** 0.25
BETA = (8 * DEPTH) ** -0.25
LN_EPS = 1e-5

kernel_name = 'hybrid_diffattn_conformer_sgu_shortconv_hmoe'


def layer_norm(x, g, b):
    xf = x.astype(jnp.float32)
    mu = jnp.mean(xf, axis=-1, keepdims=True)
    var = jnp.mean(jnp.square(xf - mu), axis=-1, keepdims=True)
    y = (xf - mu) * lax.rsqrt(var + LN_EPS) * g.astype(jnp.float32) + b.astype(jnp.float32)
    return y.astype(x.dtype)


def rms_norm(x, g):
    xf = x.astype(jnp.float32)
    y = xf * lax.rsqrt(jnp.mean(jnp.square(xf), axis=-1, keepdims=True) + LN_EPS) * g.astype(jnp.float32)
    return y.astype(x.dtype)


def rope_tables(positions, dtype):
    inv_freq = ROPE_THETA ** (-jnp.arange(0, ROPE_DIM, 2, dtype=jnp.float32) / ROPE_DIM)
    ang = positions.astype(jnp.float32)[..., None] * inv_freq
    return jnp.cos(ang)[:, :, None, :].astype(dtype), jnp.sin(ang)[:, :, None, :].astype(dtype)


def partial_rope(t, cos, sin):
    half = ROPE_DIM // 2
    t1 = t[..., :half]
    t2 = t[..., half:ROPE_DIM]
    return jnp.concatenate([t1 * cos - t2 * sin, t2 * cos + t1 * sin, t[..., ROPE_DIM:]], axis=-1)


def causal_depthwise_conv(x, w):
    k = w.shape[0]
    xp = jnp.pad(x, ((0, 0), (k - 1, 0), (0, 0)))
    return lax.conv_general_dilated(xp, w[:, None, :].astype(x.dtype), window_strides=(1,), padding='VALID',
                                    dimension_numbers=('NWC', 'WIO', 'NWC'), feature_group_count=x.shape[-1])


def diff_attention(q1, q2, k1, k2, v, lam):
    bsz, heads, seq, dh = q1.shape
    nb = seq // Q_BLOCK
    scale = dh ** -0.5
    kpos = jnp.arange(seq)

    def to_blocks(q):
        return q.reshape(bsz, heads, nb, Q_BLOCK, dh).transpose(2, 0, 1, 3, 4)

    def one_block(args):
        i, a1, a2 = args
        qpos = i * Q_BLOCK + jnp.arange(Q_BLOCK)
        mask = kpos[None, :] <= qpos[:, None]

        def attn_map(a, k):
            s = jnp.einsum('bhqd,bhkd->bhqk', a, k).astype(jnp.float32) * scale
            return jax.nn.softmax(jnp.where(mask, s, -jnp.inf), axis=-1)

        p = attn_map(a1, k1) - lam * attn_map(a2, k2)
        return jnp.einsum('bhqk,bhkd->bhqd', p.astype(v.dtype), v)

    out = lax.map(one_block, (jnp.arange(nb), to_blocks(q1), to_blocks(q2)))
    return out.transpose(1, 2, 0, 3, 4).reshape(bsz, heads, seq, v.shape[-1])


def even_mixer(h, cos, sin, layer_idx, lq1, lk1, lq2, lk2, subln_g, conv_w, conv_b, cln_g, cln_b):
    bsz, seq, _ = h.shape
    hd = DIFF_HEADS
    q = h[..., 0:HALF_MIX].reshape(bsz, seq, 2 * hd, DIFF_HEAD_DIM)
    k = h[..., HALF_MIX:2 * HALF_MIX].reshape(bsz, seq, 2 * hd, DIFF_HEAD_DIM)
    v = h[..., 2 * HALF_MIX:3 * HALF_MIX].reshape(bsz, seq, hd, DIFF_V_DIM).transpose(0, 2, 1, 3)
    glu_in = h[..., 3 * HALF_MIX:]

    def split_maps(t):
        t = partial_rope(t, cos, sin).transpose(0, 2, 1, 3).reshape(bsz, hd, 2, seq, DIFF_HEAD_DIM)
        return t[:, :, 0], t[:, :, 1]

    q1, q2 = split_maps(q)
    k1, k2 = split_maps(k)
    lam_init = 0.8 - 0.6 * math.exp(-0.3 * layer_idx)
    lam = (jnp.exp(jnp.sum(lq1.astype(jnp.float32) * lk1.astype(jnp.float32)))
           - jnp.exp(jnp.sum(lq2.astype(jnp.float32) * lk2.astype(jnp.float32))) + lam_init)
    o = diff_attention(q1, q2, k1, k2, v, lam)
    o = rms_norm(o, subln_g) * (1.0 - lam_init)
    o = o.transpose(0, 2, 1, 3).reshape(bsz, seq, hd * DIFF_V_DIM)
    a, g = jnp.split(glu_in, 2, axis=-1)
    c = a * jax.nn.sigmoid(g)
    c = causal_depthwise_conv(c, conv_w) + conv_b.astype(c.dtype)
    c = jax.nn.silu(layer_norm(c, cln_g, cln_b))
    return jnp.concatenate([o, c], axis=-1)


def odd_mixer(h, ln_g, ln_b, sgu_w, sgu_b, sc_w):
    bsz, seq, _ = h.shape
    nc = seq // SGU_CHUNK
    z = jax.nn.gelu(h[..., :2 * SGU_WIDTH], approximate=False)
    u, vg = jnp.split(z, 2, axis=-1)
    vg = layer_norm(vg, ln_g, ln_b).reshape(bsz, nc, SGU_CHUNK, SGU_GROUPS, SGU_GROUP_DIM)
    w_causal = sgu_w * jnp.tril(jnp.ones((SGU_CHUNK, SGU_CHUNK), sgu_w.dtype))
    sv = jnp.einsum('gts,bnsgc->bntgc', w_causal, vg) + sgu_b.T[:, :, None].astype(vg.dtype)
    spatial = u * sv.reshape(bsz, seq, SGU_WIDTH)
    gb, gc, xin = jnp.split(h[..., 2 * SGU_WIDTH:], 3, axis=-1)
    conv = gb * causal_depthwise_conv(gc * xin, sc_w)
    return jnp.concatenate([spatial, conv], axis=-1)


def memory_cross_attention(x, mem_k, mem_v, wq, wo):
    bsz, seq, _ = x.shape
    q = (x @ wq).reshape(bsz, seq, CROSS_HEADS, CROSS_HEAD_DIM)
    s = jnp.einsum('bshd,bmhd->bhsm', q, mem_k).astype(jnp.float32) * CROSS_HEAD_DIM ** -0.5
    p = jax.nn.softmax(s, axis=-1).astype(x.dtype)
    o = jnp.einsum('bhsm,bmhd->bshd', p, mem_v).reshape(bsz, seq, CROSS_WIDTH)
    return o @ wo


def hier_moe(x, rg_w, rg_b, re_w, re_b, w1, w3, w2):
    bsz, seq, d = x.shape
    xt = x.reshape(-1, d)
    n_tok = xt.shape[0]
    g_logits = (xt @ rg_w + rg_b).astype(jnp.float32)
    grp = jnp.argmax(g_logits, axis=-1)
    g_gate = jnp.take_along_axis(jax.nn.softmax(g_logits, axis=-1), grp[:, None], axis=1)[:, 0]
    e_logits = (xt @ re_w + re_b).astype(jnp.float32).reshape(n_tok, N_GROUPS, EXPERTS_PER_GROUP)
    e_logits = jnp.take_along_axis(e_logits, grp[:, None, None], axis=1)[:, 0]
    top_v, top_i = lax.top_k(e_logits, TOP_K)
    gate = jax.nn.softmax(top_v, axis=-1) * g_gate[:, None]
    eid = (grp[:, None] * EXPERTS_PER_GROUP + top_i).reshape(-1).astype(jnp.int32)
    n_assign = eid.shape[0]
    tok = jnp.repeat(jnp.arange(n_tok, dtype=jnp.int32), TOP_K)
    wgt = gate.reshape(-1)
    order = jnp.argsort(eid, stable=True)
    se, stok, sw = eid[order], tok[order], wgt[order]
    counts = jnp.bincount(eid, length=N_EXPERTS)
    padded = (counts + MOE_BLOCK - 1) // MOE_BLOCK * MOE_BLOCK
    start = jnp.cumsum(counts) - counts
    pend = jnp.cumsum(padded)
    pstart = pend - padded
    dest = pstart[se] + jnp.arange(n_assign, dtype=jnp.int32) - start[se]
    n_blocks = -(-n_assign // MOE_BLOCK) + N_EXPERTS
    buf_tok = jnp.zeros((n_blocks * MOE_BLOCK,), jnp.int32).at[dest].set(stok)
    buf_w = jnp.zeros((n_blocks * MOE_BLOCK,), jnp.float32).at[dest].set(sw)
    blk_e = jnp.minimum(jnp.searchsorted(pend, jnp.arange(n_blocks, dtype=jnp.int32) * MOE_BLOCK, side='right'),
                        N_EXPERTS - 1)

    def run(args):
        e, ti, wi = args
        xb = xt[ti]
        hb = jax.nn.silu(xb @ w1[e]) * (xb @ w3[e])
        return (hb @ w2[e]) * wi[:, None].astype(xb.dtype)

    y = lax.map(run, (blk_e, buf_tok.reshape(n_blocks, MOE_BLOCK), buf_w.reshape(n_blocks, MOE_BLOCK)))
    out = jax.ops.segment_sum(y.reshape(-1, d), buf_tok, num_segments=n_tok)
    return out.reshape(bsz, seq, d)


def setup_inputs(seed: int = 0) -> dict:
    key = jax.random.key(seed)
    ks = iter(jax.random.split(key, 40))

    def nrm(shape, scale):
        return jax.random.normal(next(ks), shape, jnp.float32) * scale

    L = DEPTH
    x = nrm((BATCH, SEQ, D_MODEL), 1.0)
    mem = nrm((BATCH, MEM_LEN, D_MODEL), 1.0)
    positions = (jax.random.randint(next(ks), (BATCH, 1), 0, 4096, jnp.int32)
                 + jnp.arange(SEQ, dtype=jnp.int32)[None, :])
    w_in = nrm((L, D_MODEL, IN_WIDTH), D_MODEL ** -0.5)
    w_in = w_in.at[0::2, :, 2 * HALF_MIX:3 * HALF_MIX].multiply(BETA)
    w_out = nrm((L, MIX_WIDTH, D_MODEL), BETA * MIX_WIDTH ** -0.5)
    ln_mix_g = 1.0 + nrm((L, D_MODEL), 0.02)
    ln_mix_b = nrm((L, D_MODEL), 0.02)
    ln_mem_g = 1.0 + nrm((L, D_MODEL), 0.02)
    ln_mem_b = nrm((L, D_MODEL), 0.02)
    ln_ffn_g = 1.0 + nrm((L, D_MODEL), 0.02)
    ln_ffn_b = nrm((L, D_MODEL), 0.02)
    lam_q1 = nrm((N_EVEN, DIFF_HEAD_DIM), 0.1)
    lam_k1 = nrm((N_EVEN, DIFF_HEAD_DIM), 0.1)
    lam_q2 = nrm((N_EVEN, DIFF_HEAD_DIM), 0.1)
    lam_k2 = nrm((N_EVEN, DIFF_HEAD_DIM), 0.1)
    diff_subln_g = 1.0 + nrm((N_EVEN, DIFF_V_DIM), 0.02)
    conv_w = nrm((N_EVEN, CONF_KERNEL, CONF_WIDTH), CONF_KERNEL ** -0.5)
    conv_b = nrm((N_EVEN, CONF_WIDTH), 0.02)
    conv_ln_g = 1.0 + nrm((N_EVEN, CONF_WIDTH), 0.02)
    conv_ln_b = nrm((N_EVEN, CONF_WIDTH), 0.02)
    sgu_ln_g = 1.0 + nrm((N_ODD, SGU_WIDTH), 0.02)
    sgu_ln_b = nrm((N_ODD, SGU_WIDTH), 0.02)
    sgu_w = nrm((N_ODD, SGU_GROUPS, SGU_CHUNK, SGU_CHUNK), SGU_CHUNK ** -0.5)
    sgu_b = 1.0 + nrm((N_ODD, SGU_GROUPS, SGU_CHUNK), 0.02)
    sc_w = nrm((N_ODD, SC_KERNEL, SC_WIDTH), SC_KERNEL ** -0.5)
    mem_kv_w = nrm((D_MODEL, 2 * CROSS_WIDTH), D_MODEL ** -0.5)
    mem_kv_w = mem_kv_w.at[:, CROSS_WIDTH:].multiply(BETA)
    xq_w = nrm((L, D_MODEL, CROSS_WIDTH), D_MODEL ** -0.5)
    xo_w = nrm((L, CROSS_WIDTH, D_MODEL), BETA * CROSS_WIDTH ** -0.5)
    rg_w = nrm((L, D_MODEL, N_GROUPS), D_MODEL ** -0.5)
    rg_b = nrm((L, N_GROUPS), 0.01)
    re_w = nrm((L, D_MODEL, N_EXPERTS), D_MODEL ** -0.5)
    re_b = nrm((L, N_EXPERTS), 0.01)
    e_w1 = nrm((L, N_EXPERTS, D_MODEL, EXPERT_FF), D_MODEL ** -0.5)
    e_w3 = nrm((L, N_EXPERTS, D_MODEL, EXPERT_FF), D_MODEL ** -0.5)
    e_w2 = nrm((L, N_EXPERTS, EXPERT_FF, D_MODEL), BETA * EXPERT_FF ** -0.5)
    return {'x': x, 'mem': mem, 'positions': positions, 'w_in': w_in, 'w_out': w_out,
            'ln_mix_g': ln_mix_g, 'ln_mix_b': ln_mix_b, 'ln_mem_g': ln_mem_g, 'ln_mem_b': ln_mem_b,
            'ln_ffn_g': ln_ffn_g, 'ln_ffn_b': ln_ffn_b,
            'lam_q1': lam_q1, 'lam_k1': lam_k1, 'lam_q2': lam_q2, 'lam_k2': lam_k2, 'diff_subln_g': diff_subln_g,
            'conv_w': conv_w, 'conv_b': conv_b, 'conv_ln_g': conv_ln_g, 'conv_ln_b': conv_ln_b,
            'sgu_ln_g': sgu_ln_g, 'sgu_ln_b': sgu_ln_b, 'sgu_w': sgu_w, 'sgu_b': sgu_b, 'sc_w': sc_w,
            'mem_kv_w': mem_kv_w, 'xq_w': xq_w, 'xo_w': xo_w,
            'rg_w': rg_w, 'rg_b': rg_b, 're_w': re_w, 're_b': re_b, 'e_w1': e_w1, 'e_w3': e_w3, 'e_w2': e_w2}


def reference(x, mem, positions, w_in, w_out, ln_mix_g, ln_mix_b, ln_mem_g, ln_mem_b, ln_ffn_g, ln_ffn_b,
              lam_q1, lam_k1, lam_q2, lam_k2, diff_subln_g, conv_w, conv_b, conv_ln_g, conv_ln_b,
              sgu_ln_g, sgu_ln_b, sgu_w, sgu_b, sc_w, mem_kv_w, xq_w, xo_w,
              rg_w, rg_b, re_w, re_b, e_w1, e_w3, e_w2):
    bsz = x.shape[0]
    n_mem = mem.shape[1]
    cos, sin = rope_tables(positions, x.dtype)
    kv = (mem @ mem_kv_w).reshape(bsz, n_mem, 2, CROSS_HEADS, CROSS_HEAD_DIM)
    mem_k, mem_v = kv[:, :, 0], kv[:, :, 1]
    for l in range(DEPTH):
        h = x @ w_in[l]
        j = l // 2
        if l % 2 == 0:
            mix = even_mixer(h, cos, sin, l, lam_q1[j], lam_k1[j], lam_q2[j], lam_k2[j], diff_subln_g[j],
                             conv_w[j], conv_b[j], conv_ln_g[j], conv_ln_b[j])
        else:
            mix = odd_mixer(h, sgu_ln_g[j], sgu_ln_b[j], sgu_w[j], sgu_b[j], sc_w[j])
        x = layer_norm(ALPHA * x + mix @ w_out[l], ln_mix_g[l], ln_mix_b[l])
        x = layer_norm(ALPHA * x + memory_cross_attention(x, mem_k, mem_v, xq_w[l], xo_w[l]),
                       ln_mem_g[l], ln_mem_b[l])
        x = layer_norm(ALPHA * x + hier_moe(x, rg_w[l], rg_b[l], re_w[l], re_b[l], e_w1[l], e_w3[l], e_w2[l]),
                       ln_ffn_g[l], ln_ffn_b[l])
    return x
```

```python
import functools
import math

import jax
import jax.numpy as jnp
from jax import lax
from jax.experimental import pallas as pl
from jax.experimental.pallas import tpu as pltpu

F32 = jnp.float32
BF16 = jnp.bfloat16

DIFF_HEADS = 4
ROPE_THETA = 500000.0
CONF_KERNEL = 31
SGU_CHUNK = 128
CROSS_HEADS = 4
N_GROUPS = 4
TOP_K = 2
LN_EPS = 1e-5

LANES = 128
SUBLANES = 8
NEG_BIG = -1e30
VMEM_LIMIT = 56 * 1024 * 1024

TM_PROJ = 512
TN_PROJ = 1024
TQ_ATT = 1024
TK_ATT = 1024
TR_MIX = 256
CONV_HALO = 32
CONV_RB = 32
SC_HALO = 16
TB_MOE = 256


def _cparams(n_axes):
    return pltpu.CompilerParams(dimension_semantics=("arbitrary",) * n_axes,
                                vmem_limit_bytes=VMEM_LIMIT)


def _layer_norm(y, g, b):
    mu = jnp.mean(y, axis=-1, keepdims=True)
    d = y - mu
    var = jnp.mean(d * d, axis=-1, keepdims=True)
    return d * lax.rsqrt(var + LN_EPS) * g + b


def _inproj_kernel(x_ref, w_ref, *rest, n_rope_tiles, n_scale_tiles, scale):
    if n_rope_tiles:
        c_ref, s1_ref, s2_ref, o_ref, xb_ref = rest
    else:
        o_ref, xb_ref = rest
    j = pl.program_id(1)

    @pl.when(j == 0)
    def _():
        xb_ref[...] = x_ref[...].astype(BF16)

    acc = jnp.dot(xb_ref[...], w_ref[...], preferred_element_type=F32)
    if not n_rope_tiles:
        o_ref[...] = acc.astype(o_ref.dtype)
        return

    @pl.when(j >= n_rope_tiles)
    def _():
        o_ref[...] = acc.astype(o_ref.dtype)

    @pl.when(j < n_rope_tiles)
    def _():
        a = acc * jnp.where(j < n_scale_tiles, scale, 1.0).astype(F32)
        c, s1, s2 = c_ref[...], s1_ref[...], s2_ref[...]
        for g in range(a.shape[1] // LANES):
            ag = a[:, g * LANES:(g + 1) * LANES]
            og = (ag * c + pltpu.roll(ag, LANES - ROPE_HALF, axis=1) * s1
                  + pltpu.roll(ag, ROPE_HALF, axis=1) * s2)
            o_ref[:, g * LANES:(g + 1) * LANES] = og.astype(o_ref.dtype)


ROPE_HALF = 16


def _inproj(x, w_bf16, rope=None, *, n_rope_tiles=0, n_scale_tiles=0, scale=1.0, out_dtype=BF16):
    m, k = x.shape
    n = w_bf16.shape[1]
    tm = min(TM_PROJ, m)
    tn = min(TN_PROJ, n)
    assert m % tm == 0 and n % tn == 0
    in_specs = [pl.BlockSpec((tm, k), lambda i, j: (i, 0)),
                pl.BlockSpec((k, tn), lambda i, j: (0, j))]
    args = [x, w_bf16]
    if n_rope_tiles:
        in_specs += [pl.BlockSpec((tm, LANES), lambda i, j: (i, 0))] * 3
        args += list(rope)
    return pl.pallas_call(
        functools.partial(_inproj_kernel, n_rope_tiles=n_rope_tiles, n_scale_tiles=n_scale_tiles, scale=scale),
        grid=(m // tm, n // tn),
        in_specs=in_specs,
        out_specs=pl.BlockSpec((tm, tn), lambda i, j: (i, j)),
        out_shape=jax.ShapeDtypeStruct((m, n), out_dtype),
        scratch_shapes=[pltpu.VMEM((tm, k), BF16)],
        compiler_params=_cparams(2),
        name="inproj_rope" if n_rope_tiles else "inproj",
    )(*args)


def _diff_attn_kernel(qi_ref, ki_ref, lam_ref, q_ref, k_ref, v_ref, g_ref, o_ref, m_ref, l_ref, acc_ref,
                      *, dh, out_scale):
    p = pl.program_id(2)
    qi = qi_ref[p]
    ki = ki_ref[p]
    tq, tk = q_ref.shape[0], k_ref.shape[0]

    @pl.when(ki == 0)
    def _():
        m_ref[...] = jnp.full(m_ref.shape, NEG_BIG, F32)
        l_ref[...] = jnp.zeros(l_ref.shape, F32)
        acc_ref[...] = jnp.zeros(acc_ref.shape, F32)

    def update(masked):
        v = v_ref[...]
        if masked:
            row = lax.broadcasted_iota(jnp.int32, (tq, tk), 0)
            col = lax.broadcasted_iota(jnp.int32, (tq, tk), 1)
            visible = col <= row
        for mi in range(2):
            qm = q_ref[:, mi * dh:(mi + 1) * dh]
            km = k_ref[:, mi * dh:(mi + 1) * dh]
            s = lax.dot_general(qm, km, (((1,), (1,)), ((), ())), preferred_element_type=F32)
            if masked:
                s = jnp.where(visible, s, NEG_BIG)
            m_prev = m_ref[mi]
            m_new = jnp.maximum(m_prev, jnp.max(s, axis=1, keepdims=True))
            alpha = jnp.exp(m_prev - m_new)
            pr = jnp.exp(s - m_new[:, :1])
            l_ref[mi] = alpha * l_ref[mi] + jnp.sum(pr, axis=1, keepdims=True)
            acc_ref[mi] = alpha[:, :1] * acc_ref[mi] + jnp.dot(pr.astype(BF16), v, preferred_element_type=F32)
            m_ref[mi] = m_new

    @pl.when(ki < qi)
    def _():
        update(False)

    @pl.when(ki == qi)
    def _():
        update(True)
        lam = lam_ref[0, 0]
        o = acc_ref[0] / l_ref[0][:, :1] - lam * (acc_ref[1] / l_ref[1][:, :1])
        ms = jnp.mean(o * o, axis=-1, keepdims=True)
        o_ref[...] = (o * lax.rsqrt(ms + LN_EPS) * g_ref[...] * out_scale).astype(o_ref.dtype)


def _diff_attention(h, lam, subln_g, *, bsz, seq, out_scale):
    t = h.shape[0]
    dv = subln_g.shape[-1]
    dh = dv // 2
    tq = min(TQ_ATT, seq)
    tk = tq
    nq = seq // tq
    assert seq % tq == 0
    pairs = [(a, b) for a in range(nq) for b in range(a + 1)]
    qi_tab = jnp.asarray([a for a, _ in pairs], jnp.int32)
    ki_tab = jnp.asarray([b for _, b in pairs], jnp.int32)
    hd_n = DIFF_HEADS
    grid_spec = pltpu.PrefetchScalarGridSpec(
        num_scalar_prefetch=2,
        grid=(bsz, hd_n, len(pairs)),
        in_specs=[
            pl.BlockSpec(memory_space=pltpu.SMEM),
            pl.BlockSpec((tq, dv), lambda b, hd, p, qt, kt: (b * nq + qt[p], hd)),
            pl.BlockSpec((tk, dv), lambda b, hd, p, qt, kt: (b * nq + kt[p], hd_n + hd)),
            pl.BlockSpec((tk, dv), lambda b, hd, p, qt, kt: (b * nq + kt[p], 2 * hd_n + hd)),
            pl.BlockSpec((1, dv), lambda b, hd, p, qt, kt: (0, 0)),
        ],
        out_specs=pl.BlockSpec((tq, dv), lambda b, hd, p, qt, kt: (b * nq + qt[p], hd)),
        scratch_shapes=[pltpu.VMEM((2, tq, LANES), F32), pltpu.VMEM((2, tq, LANES), F32),
                        pltpu.VMEM((2, tq, dv), F32)],
    )
    return pl.pallas_call(
        functools.partial(_diff_attn_kernel, dh=dh, out_scale=out_scale),
        grid_spec=grid_spec,
        out_shape=jax.ShapeDtypeStruct((t, hd_n * dv), BF16),
        compiler_params=_cparams(3),
        name="diff_attn",
    )(qi_tab, ki_tab, lam.reshape(1, 1).astype(F32), h, h, h, subln_g.reshape(1, dv).astype(F32))


def _conformer_kernel(a_ref, g_ref, ah_ref, gh_ref, w_ref, cb_ref, lg_ref, lb_ref, o_ref, cext_ref, conv_ref,
                      *, tiles_per_seq):
    i = pl.program_id(0)
    tr, width = a_ref.shape
    halo = ah_ref.shape[0]
    ksz = w_ref.shape[0]
    first = (i % tiles_per_seq) == 0

    glu_h = ah_ref[...].astype(F32) * jax.nn.sigmoid(gh_ref[...].astype(F32))
    cext_ref[0:halo, :] = jnp.where(first, 0.0, glu_h)
    cext_ref[halo:halo + tr, :] = a_ref[...].astype(F32) * jax.nn.sigmoid(g_ref[...].astype(F32))
    cext_ref[halo + tr:halo + tr + SUBLANES, :] = jnp.zeros((SUBLANES, width), F32)

    base = halo - (ksz - 1)
    win = CONV_RB + halo + SUBLANES

    def chunk(rc, carry):
        r0 = pl.multiple_of(rc * CONV_RB, CONV_RB)
        for c in range(width // LANES):
            lanes = slice(c * LANES, (c + 1) * LANES)
            wnd = cext_ref[pl.ds(r0, win), lanes]
            acc = jnp.zeros((CONV_RB, LANES), F32)
            for b in range(SUBLANES):
                shifted = wnd if b == 0 else pltpu.roll(wnd, win - b, axis=0)
                for a in range((halo + SUBLANES) // SUBLANES):
                    j = SUBLANES * a + b - base
                    if 0 <= j < ksz:
                        acc = acc + w_ref[j:j + 1, lanes] * shifted[SUBLANES * a:SUBLANES * a + CONV_RB]
            conv_ref[pl.ds(r0, CONV_RB), lanes] = acc + cb_ref[:, lanes]
        return carry

    lax.fori_loop(0, tr // CONV_RB, chunk, 0)
    y = _layer_norm(conv_ref[...], lg_ref[...], lb_ref[...])
    o_ref[...] = (y * jax.nn.sigmoid(y)).astype(o_ref.dtype)


def _conformer(h, conv_w, conv_b, ln_g, ln_b, *, seq, col_a, col_g):
    t = h.shape[0]
    ksz, width = conv_w.shape
    tr = min(TR_MIX, seq)
    halo = CONV_HALO
    assert seq % tr == 0 and tr % halo == 0 and ksz - 1 <= halo and tr % CONV_RB == 0
    rpb = tr // halo
    row = lambda v: v.reshape(1, width).astype(F32)
    return pl.pallas_call(
        functools.partial(_conformer_kernel, tiles_per_seq=seq // tr),
        grid=(t // tr,),
        in_specs=[
            pl.BlockSpec((tr, width), lambda i: (i, col_a)),
            pl.BlockSpec((tr, width), lambda i: (i, col_g)),
            pl.BlockSpec((halo, width), lambda i: (jnp.maximum(i * rpb - 1, 0), col_a)),
            pl.BlockSpec((halo, width), lambda i: (jnp.maximum(i * rpb - 1, 0), col_g)),
            pl.BlockSpec((ksz, width), lambda i: (0, 0)),
            pl.BlockSpec((1, width), lambda i: (0, 0)),
            pl.BlockSpec((1, width), lambda i: (0, 0)),
            pl.BlockSpec((1, width), lambda i: (0, 0)),
        ],
        out_specs=pl.BlockSpec((tr, width), lambda i: (i, 0)),
        out_shape=jax.ShapeDtypeStruct((t, width), BF16),
        scratch_shapes=[pltpu.VMEM((halo + tr + SUBLANES, width), F32), pltpu.VMEM((tr, width), F32)],
        compiler_params=_cparams(1),
        name="conformer_conv",
    )(h, h, h, h, conv_w.astype(F32), row(conv_b), row(ln_g), row(ln_b))


def _gelu_exact(x):
    return 0.5 * x * (1.0 + lax.erf(x * math.sqrt(0.5)))


def _sgu_kernel(u_ref, v_ref, gb_ref, gc_ref, xi_ref, gch_ref, xih_ref, lg_ref, lb_ref, sw_ref, sb_ref, cw_ref,
                o_ref, pext_ref, *, tiles_per_seq):
    i = pl.program_id(0)
    tr, width = u_ref.shape
    halo = gch_ref.shape[0]
    n_grp, chunk, _ = sw_ref.shape
    gdim = width // n_grp
    first = (i % tiles_per_seq) == 0

    vg = _layer_norm(_gelu_exact(v_ref[...].astype(F32)), lg_ref[...], lb_ref[...]).astype(BF16)
    trow = lax.broadcasted_iota(jnp.int32, (chunk, chunk), 0)
    tcol = lax.broadcasted_iota(jnp.int32, (chunk, chunk), 1)
    for g in range(n_grp):
        wg = jnp.where(tcol <= trow, sw_ref[g], 0.0).astype(BF16)
        for n in range(tr // chunk):
            rows = slice(n * chunk, (n + 1) * chunk)
            cols = slice(g * gdim, (g + 1) * gdim)
            sv = jnp.dot(wg, vg[rows, cols], preferred_element_type=F32) + sb_ref[g]
            o_ref[rows, cols] = (_gelu_exact(u_ref[rows, cols].astype(F32)) * sv).astype(o_ref.dtype)

    ph = gch_ref[...].astype(F32) * xih_ref[...].astype(F32)
    pext_ref[0:halo, :] = jnp.where(first, 0.0, ph)
    pext_ref[halo:halo + tr, :] = gc_ref[...].astype(F32) * xi_ref[...].astype(F32)
    pe = pext_ref[...]
    n_ext = halo + tr
    ksz = cw_ref.shape[0]
    conv = cw_ref[ksz - 1:ksz, :] * pe[halo:]
    for back in range(1, ksz):
        conv = conv + cw_ref[ksz - 1 - back:ksz - back, :] * pltpu.roll(pe, back, axis=0)[halo:]
    del n_ext
    o_ref[:, width:2 * width] = (gb_ref[...].astype(F32) * conv).astype(o_ref.dtype)


def _sgu_shortconv(h, ln_g, ln_b, sgu_w, sgu_b, sc_w, *, seq):
    t = h.shape[0]
    width = ln_g.shape[-1]
    n_grp, chunk, _ = sgu_w.shape
    gdim = width // n_grp
    tr = min(TR_MIX, seq)
    halo = SC_HALO
    assert seq % tr == 0 and tr % chunk == 0 and tr % halo == 0 and sc_w.shape[0] - 1 <= halo
    rpb = tr // halo
    row = lambda v: v.reshape(1, width).astype(F32)
    bias = jnp.broadcast_to(sgu_b.astype(F32)[:, :, None], (n_grp, chunk, gdim))
    cur = lambda c: pl.BlockSpec((tr, width), lambda i: (i, c))
    prev = lambda c: pl.BlockSpec((halo, width), lambda i: (jnp.maximum(i * rpb - 1, 0), c))
    whole = lambda a: pl.BlockSpec(a.shape, lambda i: (0,) * a.ndim)
    small = [row(ln_g), row(ln_b), sgu_w.astype(F32), bias, sc_w.astype(F32)]
    return pl.pallas_call(
        functools.partial(_sgu_kernel, tiles_per_seq=seq // tr),
        grid=(t // tr,),
        in_specs=[cur(0), cur(1), cur(2), cur(3), cur(4), prev(3), prev(4)] + [whole(a) for a in small],
        out_specs=pl.BlockSpec((tr, 2 * width), lambda i: (i, 0)),
        out_shape=jax.ShapeDtypeStruct((t, 2 * width), BF16),
        scratch_shapes=[pltpu.VMEM((halo + tr, width), F32)],
        compiler_params=_cparams(1),
        name="sgu_shortconv",
    )(h, h, h, h, h, h, h, *small)


def _outproj_ln_kernel(*refs, n_parts, alpha):
    parts = refs[:n_parts]
    ws = refs[n_parts:2 * n_parts]
    x_ref, g_ref, b_ref, o_ref = refs[2 * n_parts:]
    acc = alpha * x_ref[...]
    for p_ref, w_ref in zip(parts, ws):
        acc = acc + jnp.dot(p_ref[...], w_ref[...], preferred_element_type=F32)
    o_ref[...] = _layer_norm(acc, g_ref[...], b_ref[...])


def _outproj_ln(parts, w_bf16, x, g, b, *, alpha):
    t, d = x.shape
    tm = min(TM_PROJ, t)
    assert t % tm == 0
    in_specs, ws, off = [], [], 0
    for p in parts:
        kp = p.shape[1]
        in_specs.append(pl.BlockSpec((tm, kp), lambda i: (i, 0)))
        ws.append(w_bf16[off:off + kp])
        off += kp
    assert off == w_bf16.shape[0]
    in_specs += [pl.BlockSpec(w.shape, lambda i: (0, 0)) for w in ws]
    in_specs += [pl.BlockSpec((tm, d), lambda i: (i, 0)),
                 pl.BlockSpec((1, d), lambda i: (0, 0)), pl.BlockSpec((1, d), lambda i: (0, 0))]
    return pl.pallas_call(
        functools.partial(_outproj_ln_kernel, n_parts=len(parts), alpha=alpha),
        grid=(t // tm,),
        in_specs=in_specs,
        out_specs=pl.BlockSpec((tm, d), lambda i: (i, 0)),
        out_shape=jax.ShapeDtypeStruct((t, d), F32),
        compiler_params=_cparams(1),
        name="outproj_ln",
    )(*parts, *ws, x, g.reshape(1, d).astype(F32), b.reshape(1, d).astype(F32))


def _cross_router_kernel(x_ref, wq_ref, k_ref, v_ref, wo_ref, g_ref, b_ref, wr_ref, br_ref, o_ref, r_ref,
                         *, alpha, scale, n_heads, n_groups, per_group):
    x = x_ref[...]
    q = (jnp.dot(x.astype(BF16), wq_ref[...], preferred_element_type=F32) * scale).astype(BF16)
    dh = q.shape[1] // n_heads
    outs = []
    for hh in range(n_heads):
        cols = slice(hh * dh, (hh + 1) * dh)
        s = lax.dot_general(q[:, cols], k_ref[:, cols], (((1,), (1,)), ((), ())), preferred_element_type=F32)
        e = jnp.exp(s - jnp.max(s, axis=-1, keepdims=True))
        pr = e / jnp.sum(e, axis=-1, keepdims=True)
        outs.append(jnp.dot(pr.astype(BF16), v_ref[:, cols], preferred_element_type=F32))
    o = jnp.concatenate(outs, axis=-1).astype(BF16)
    y = _layer_norm(alpha * x + jnp.dot(o, wo_ref[...], preferred_element_type=F32), g_ref[...], b_ref[...])
    o_ref[...] = y

    logits = jnp.dot(y.astype(BF16), wr_ref[...], preferred_element_type=F32) + br_ref[...]
    lane = lax.broadcasted_iota(jnp.int32, logits.shape, 1)
    far = jnp.int32(LANES)
    is_g = lane < n_groups
    gl = jnp.where(is_g, logits, -jnp.inf)
    gmax = jnp.max(gl, axis=-1, keepdims=True)
    grp = jnp.min(jnp.where(is_g & (gl == gmax), lane, far), axis=-1, keepdims=True)
    g_gate = 1.0 / jnp.sum(jnp.where(is_g, jnp.exp(gl - gmax), 0.0), axis=-1, keepdims=True)
    lo = n_groups + per_group * grp
    is_e = (lane >= lo) & (lane < lo + per_group)
    el = jnp.where(is_e, logits, -jnp.inf)
    v1 = jnp.max(el, axis=-1, keepdims=True)
    i1 = jnp.min(jnp.where(is_e & (el == v1), lane, far), axis=-1, keepdims=True)
    is_e2 = is_e & (lane != i1)
    el2 = jnp.where(is_e2, logits, -jnp.inf)
    v2 = jnp.max(el2, axis=-1, keepdims=True)
    i2 = jnp.min(jnp.where(is_e2 & (el2 == v2), lane, far), axis=-1, keepdims=True)
    e2 = jnp.exp(v2 - v1)
    w1 = g_gate / (1.0 + e2)
    w2 = g_gate * e2 / (1.0 + e2)
    r = jnp.where(lane == 0, (i1 - n_groups).astype(F32),
                  jnp.where(lane == 1, (i2 - n_groups).astype(F32),
                            jnp.where(lane == 2, w1, jnp.where(lane == 3, w2, 0.0))))
    r_ref[...] = r


def _cross_router(x, kv_bf16, wq_bf16, wo_bf16, g, b, wr_bf16, br, *, seq, mem_len, alpha, per_group):
    t, d = x.shape
    cw = wq_bf16.shape[1]
    tm = min(TM_PROJ, seq)
    assert seq % tm == 0
    tps = seq // tm
    dh = cw // CROSS_HEADS
    n_exp = wr_bf16.shape[1]
    return pl.pallas_call(
        functools.partial(_cross_router_kernel, alpha=alpha, scale=dh ** -0.5, n_heads=CROSS_HEADS,
                          n_groups=N_GROUPS, per_group=per_group),
        grid=(t // tm,),
        in_specs=[
            pl.BlockSpec((tm, d), lambda i: (i, 0)),
            pl.BlockSpec((d, cw), lambda i: (0, 0)),
            pl.BlockSpec((mem_len, cw), lambda i: (i // tps, 0)),
            pl.BlockSpec((mem_len, cw), lambda i: (i // tps, 1)),
            pl.BlockSpec((cw, d), lambda i: (0, 0)),
            pl.BlockSpec((1, d), lambda i: (0, 0)),
            pl.BlockSpec((1, d), lambda i: (0, 0)),
            pl.BlockSpec((d, n_exp), lambda i: (0, 0)),
            pl.BlockSpec((1, n_exp), lambda i: (0, 0)),
        ],
        out_specs=[pl.BlockSpec((tm, d), lambda i: (i, 0)), pl.BlockSpec((tm, LANES), lambda i: (i, 0))],
        out_shape=[jax.ShapeDtypeStruct((t, d), F32), jax.ShapeDtypeStruct((t, LANES), F32)],
        compiler_params=_cparams(1),
        name="cross_attn_router",
    )(x, wq_bf16, kv_bf16, kv_bf16, wo_bf16, g.reshape(1, d).astype(F32), b.reshape(1, d).astype(F32),
      wr_bf16, br)


def _moe_kernel(be_ref, nu_ref, tok0_ref, tokn_ref, dst_ref, gw_ref, w1_ref, w3_ref, w2_ref, x_hbm, o_hbm,
                xbuf, ybuf, gsem, ssem):
    i = pl.program_id(0)
    n_used = nu_ref[0]
    tb = xbuf.shape[1]
    slot = i % 2
    other = 1 - slot

    def gather_copy(tok, r, s):
        return pltpu.make_async_copy(x_hbm.at[pl.ds(tok, 1), :], xbuf.at[s, pl.ds(r, 1), :], gsem.at[s])

    def scatter_copy(dst, r, s):
        return pltpu.make_async_copy(ybuf.at[s, pl.ds(r, 1), :], o_hbm.at[pl.ds(dst, 1), :], ssem.at[s])

    def start_gather(tok_ref, s):
        def body(r, c):
            gather_copy(tok_ref[0, 0, r], r, s).start()
            return c
        lax.fori_loop(0, tb, body, 0, unroll=8)

    def wait_rows(make, s):
        def body(r, c):
            make(0, r, s).wait()
            return c
        lax.fori_loop(0, tb, body, 0, unroll=8)

    @pl.when(i < n_used)
    def _():
        @pl.when(i == 0)
        def _():
            start_gather(tok0_ref, 0)
            ybuf[1] = jnp.zeros(ybuf.shape[1:], F32)
            spare = pltpu.make_async_copy(ybuf.at[1], o_hbm.at[pl.ds(o_hbm.shape[0] - tb, tb), :], ssem.at[1])
            spare.start()
            spare.wait()

        @pl.when(i + 1 < n_used)
        def _():
            start_gather(tokn_ref, other)

        wait_rows(gather_copy, slot)
        xb = xbuf[slot].astype(BF16)
        h1 = jnp.dot(xb, w1_ref[0], preferred_element_type=F32)
        h3 = jnp.dot(xb, w3_ref[0], preferred_element_type=F32)
        hb = (h1 * jax.nn.sigmoid(h1) * h3).astype(BF16)
        ybuf[slot] = jnp.dot(hb, w2_ref[0], preferred_element_type=F32) * gw_ref[...]

        @pl.when(i > 0)
        def _():
            wait_rows(scatter_copy, other)

        def sbody(r, c):
            scatter_copy(dst_ref[0, 0, r], r, slot).start()
            return c
        lax.fori_loop(0, tb, sbody, 0, unroll=8)

        @pl.when(i == n_used - 1)
        def _():
            wait_rows(scatter_copy, slot)


def _moe_experts(x, blk_e, n_used, tok, dst, gw, w1, w3, w2, *, n_out_rows):
    t, d = x.shape
    nb = blk_e.shape[0]
    tb = tok.shape[0] // nb
    ff = w1.shape[-1]
    tok3 = tok.reshape(nb, 1, tb)
    dst3 = dst.reshape(nb, 1, tb)
    smem_blk = lambda f: pl.BlockSpec((1, 1, tb), f, memory_space=pltpu.SMEM)
    grid_spec = pltpu.PrefetchScalarGridSpec(
        num_scalar_prefetch=2,
        grid=(nb,),
        in_specs=[
            smem_blk(lambda i, be, nu: (0, 0, 0)),
            smem_blk(lambda i, be, nu: (jnp.minimum(i + 1, nb - 1), 0, 0)),
            smem_blk(lambda i, be, nu: (i, 0, 0)),
            pl.BlockSpec((tb, 1), lambda i, be, nu: (i, 0)),
            pl.BlockSpec((1, d, ff), lambda i, be, nu: (be[i], 0, 0)),
            pl.BlockSpec((1, d, ff), lambda i, be, nu: (be[i], 0, 0)),
            pl.BlockSpec((1, ff, d), lambda i, be, nu: (be[i], 0, 0)),
            pl.BlockSpec(memory_space=pl.ANY),
        ],
        out_specs=pl.BlockSpec(memory_space=pl.ANY),
        scratch_shapes=[pltpu.VMEM((2, tb, d), F32), pltpu.VMEM((2, tb, d), F32),
                        pltpu.SemaphoreType.DMA((2,)), pltpu.SemaphoreType.DMA((2,))],
    )
    return pl.pallas_call(
        _moe_kernel,
        grid_spec=grid_spec,
        out_shape=jax.ShapeDtypeStruct((n_out_rows, d), F32),
        compiler_params=_cparams(1),
        name="moe_experts",
    )(blk_e, n_used, tok3, tok3, dst3, gw, w1, w3, w2, x)


def _moe_dispatch(route, *, n_experts, tb):
    t = route.shape[0]
    n_assign = TOP_K * t
    nb = n_assign // tb + n_experts
    eid = route[:, :TOP_K].astype(jnp.int32).reshape(-1)
    wgt = route[:, TOP_K:2 * TOP_K].reshape(-1)
    order = jnp.argsort(eid, stable=True).astype(jnp.int32)
    se = eid[order]
    counts = jnp.sum((eid[:, None] == jnp.arange(n_experts, dtype=jnp.int32)[None, :]).astype(jnp.int32), axis=0)
    blocks = (counts + tb - 1) // tb
    bend = jnp.cumsum(blocks)
    pstart = (bend - blocks) * tb
    start = jnp.cumsum(counts) - counts
    dest = pstart[se] + jnp.arange(n_assign, dtype=jnp.int32) - start[se]
    a_tok = order // TOP_K
    a_k = order % TOP_K
    rows = nb * tb
    spare = TOP_K * t + jnp.arange(rows, dtype=jnp.int32) % tb
    tok = jnp.zeros((rows,), jnp.int32).at[dest].set(a_tok)
    dst = spare.at[dest].set(a_k * t + a_tok)
    gw = jnp.zeros((rows,), F32).at[dest].set(wgt[order]).reshape(rows, 1)
    blk_e = jnp.minimum(jnp.searchsorted(bend, jnp.arange(nb, dtype=jnp.int32), side="right"),
                        n_experts - 1).astype(jnp.int32)
    n_used = bend[-1:].astype(jnp.int32)
    return blk_e, n_used, tok, dst, gw


def _combine_ln_kernel(x_ref, y0_ref, y1_ref, g_ref, b_ref, o_ref, *, alpha):
    o_ref[...] = _layer_norm(alpha * x_ref[...] + y0_ref[...] + y1_ref[...], g_ref[...], b_ref[...])


def _combine_ln(x, y, g, b, *, alpha):
    t, d = x.shape
    tm = min(TM_PROJ, t)
    assert t % tm == 0
    nt = t // tm
    return pl.pallas_call(
        functools.partial(_combine_ln_kernel, alpha=alpha),
        grid=(nt,),
        in_specs=[pl.BlockSpec((tm, d), lambda i: (i, 0)),
                  pl.BlockSpec((tm, d), lambda i: (i, 0)),
                  pl.BlockSpec((tm, d), lambda i: (i + nt, 0)),
                  pl.BlockSpec((1, d), lambda i: (0, 0)), pl.BlockSpec((1, d), lambda i: (0, 0))],
        out_specs=pl.BlockSpec((tm, d), lambda i: (i, 0)),
        out_shape=jax.ShapeDtypeStruct((t, d), F32),
        compiler_params=_cparams(1),
        name="moe_combine_ln",
    )(x, y, y, g.reshape(1, d).astype(F32), b.reshape(1, d).astype(F32))


def _rope_tables(positions, dh):
    rope_dim = dh // 4
    half = rope_dim // 2
    assert half == ROPE_HALF and dh == LANES
    inv_freq = ROPE_THETA ** (-jnp.arange(0, rope_dim, 2, dtype=F32) / rope_dim)
    ang = positions.astype(F32).reshape(-1, 1) * inv_freq[None, :]
    cos, sin = jnp.cos(ang), jnp.sin(ang)
    t = ang.shape[0]
    c = jnp.concatenate([cos, cos, jnp.ones((t, dh - rope_dim), F32)], axis=1)
    s1 = jnp.concatenate([-sin, jnp.zeros((t, dh - half), F32)], axis=1)
    s2 = jnp.concatenate([jnp.zeros((t, half), F32), sin, jnp.zeros((t, dh - rope_dim), F32)], axis=1)
    return c, s1, s2


def kernel(x, mem, positions, w_in, w_out, ln_mix_g, ln_mix_b, ln_mem_g, ln_mem_b, ln_ffn_g, ln_ffn_b, lam_q1, lam_k1, lam_q2, lam_k2, diff_subln_g, conv_w, conv_b, conv_ln_g, conv_ln_b, sgu_ln_g, sgu_ln_b, sgu_w, sgu_b, sc_w, mem_kv_w, xq_w, xo_w, rg_w, rg_b, re_w, re_b, e_w1, e_w3, e_w2):
    bsz, seq, d = x.shape
    t = bsz * seq
    depth = w_in.shape[0]
    mem_len = mem.shape[1]
    half = w_in.shape[2] // 5
    dh = lam_q1.shape[-1]
    n_experts = re_w.shape[-1]
    alpha = (2 * depth) ** 0.25
    assert half % TN_PROJ == 0 and diff_subln_g.shape[-1] == 2 * dh and half == 2 * DIFF_HEADS * dh

    xt = x.reshape(t, d)
    rope = _rope_tables(positions, dh)
    kv = _inproj(mem.reshape(bsz * mem_len, d), mem_kv_w.astype(BF16))

    for l in range(depth):
        j = l // 2
        w_l = w_in[l].astype(BF16)
        if l % 2 == 0:
            h = _inproj(xt, w_l, rope, n_rope_tiles=2 * half // TN_PROJ, n_scale_tiles=half // TN_PROJ,
                        scale=dh ** -0.5)
            lam_init = 0.8 - 0.6 * math.exp(-0.3 * l)
            lam = (jnp.exp(jnp.sum(lam_q1[j].astype(F32) * lam_k1[j].astype(F32)))
                   - jnp.exp(jnp.sum(lam_q2[j].astype(F32) * lam_k2[j].astype(F32))) + lam_init)
            o = _diff_attention(h, lam, diff_subln_g[j], bsz=bsz, seq=seq, out_scale=1.0 - lam_init)
            c = _conformer(h, conv_w[j], conv_b[j], conv_ln_g[j], conv_ln_b[j], seq=seq, col_a=3, col_g=4)
            parts = [o, c]
        else:
            h = _inproj(xt, w_l)
            parts = [_sgu_shortconv(h, sgu_ln_g[j], sgu_ln_b[j], sgu_w[j], sgu_b[j], sc_w[j], seq=seq)]
        xt = _outproj_ln(parts, w_out[l].astype(BF16), xt, ln_mix_g[l], ln_mix_b[l], alpha=alpha)

        n_route = N_GROUPS + n_experts
        wr = jnp.concatenate([rg_w[l], re_w[l], jnp.zeros((d, LANES - n_route), F32)], axis=1).astype(BF16)
        br = jnp.concatenate([rg_b[l], re_b[l], jnp.zeros((LANES - n_route,), F32)]).reshape(1, LANES).astype(F32)
        xt, route = _cross_router(xt, kv, xq_w[l].astype(BF16), xo_w[l].astype(BF16), ln_mem_g[l], ln_mem_b[l],
                                  wr, br, seq=seq, mem_len=mem_len, alpha=alpha, per_group=n_experts // N_GROUPS)

        blk_e, n_used, tok, dst, gw = _moe_dispatch(route, n_experts=n_experts, tb=TB_MOE)
        y = _moe_experts(xt, blk_e, n_used, tok, dst, gw, e_w1[l].astype(BF16), e_w3[l].astype(BF16),
                         e_w2[l].astype(BF16), n_out_rows=TOP_K * t + TB_MOE)
        xt = _combine_ln(xt, y, ln_ffn_g[l], ln_ffn_b[l], alpha=alpha)
    return xt.reshape(bsz, seq, d)
```

```python
import functools
import math

import jax
import jax.numpy as jnp
from jax import lax
from jax.experimental import pallas as pl
from jax.experimental.pallas import tpu as pltpu

F32 = jnp.float32
BF16 = jnp.bfloat16

DIFF_HEADS = 4
ROPE_THETA = 500000.0
CONF_KERNEL = 31
SGU_CHUNK = 128
CROSS_HEADS = 4
N_GROUPS = 4
TOP_K = 2
LN_EPS = 1e-5

LANES = 128
SUBLANES = 8
NEG_BIG = -1e30
VMEM_LIMIT = 56 * 1024 * 1024

TM_PROJ = 512
TN_PROJ = 1024
TQ_ATT = 512
TR_MIX = 256
CONV_HALO = 32
CONV_RB = 32
SC_HALO = 16
TB_MOE = 256


def _cparams(n_axes):
    return pltpu.CompilerParams(dimension_semantics=("arbitrary",) * n_axes,
                                vmem_limit_bytes=VMEM_LIMIT)


def _layer_norm(y, g, b):
    mu = jnp.mean(y, axis=-1, keepdims=True)
    d = y - mu
    var = jnp.mean(d * d, axis=-1, keepdims=True)
    return d * lax.rsqrt(var + LN_EPS) * g + b


def _inproj_kernel(x_ref, w_ref, *rest, n_rope_tiles, n_scale_tiles, scale):
    if n_rope_tiles:
        c_ref, s1_ref, s2_ref, o_ref, xb_ref = rest
    else:
        o_ref, xb_ref = rest
    j = pl.program_id(1)

    @pl.when(j == 0)
    def _():
        xb_ref[...] = x_ref[...].astype(BF16)

    acc = jnp.dot(xb_ref[...], w_ref[...], preferred_element_type=F32)
    if not n_rope_tiles:
        o_ref[...] = acc.astype(o_ref.dtype)
        return

    @pl.when(j >= n_rope_tiles)
    def _():
        o_ref[...] = acc.astype(o_ref.dtype)

    @pl.when(j < n_rope_tiles)
    def _():
        a = acc * jnp.where(j < n_scale_tiles, scale, 1.0).astype(F32)
        c, s1, s2 = c_ref[...], s1_ref[...], s2_ref[...]
        for g in range(a.shape[1] // LANES):
            ag = a[:, g * LANES:(g + 1) * LANES]
            og = (ag * c + pltpu.roll(ag, LANES - ROPE_HALF, axis=1) * s1
                  + pltpu.roll(ag, ROPE_HALF, axis=1) * s2)
            o_ref[:, g * LANES:(g + 1) * LANES] = og.astype(o_ref.dtype)


ROPE_HALF = 16


def _inproj(x, w_bf16, rope=None, *, n_rope_tiles=0, n_scale_tiles=0, scale=1.0, out_dtype=BF16):
    m, k = x.shape
    n = w_bf16.shape[1]
    tm = min(TM_PROJ, m)
    tn = min(TN_PROJ, n)
    assert m % tm == 0 and n % tn == 0
    in_specs = [pl.BlockSpec((tm, k), lambda i, j: (i, 0)),
                pl.BlockSpec((k, tn), lambda i, j: (0, j))]
    args = [x, w_bf16]
    if n_rope_tiles:
        in_specs += [pl.BlockSpec((tm, LANES), lambda i, j: (i, 0))] * 3
        args += list(rope)
    return pl.pallas_call(
        functools.partial(_inproj_kernel, n_rope_tiles=n_rope_tiles, n_scale_tiles=n_scale_tiles, scale=scale),
        grid=(m // tm, n // tn),
        in_specs=in_specs,
        out_specs=pl.BlockSpec((tm, tn), lambda i, j: (i, j)),
        out_shape=jax.ShapeDtypeStruct((m, n), out_dtype),
        scratch_shapes=[pltpu.VMEM((tm, k), BF16)],
        compiler_params=_cparams(2),
        name="inproj_rope" if n_rope_tiles else "inproj",
    )(*args)


def _diff_attn_kernel(lam_ref, q_ref, k_ref, v_ref, g_ref, o_ref, s_a, s_b, x_a, x_b, m_ref, l_ref, acc_ref,
                      *, dh, out_scale):
    qi = pl.program_id(2)
    tq = q_ref.shape[0]
    tk = tq
    dv = v_ref.shape[1]
    ngrp = tk // LANES
    m_ref[...] = jnp.full(m_ref.shape, NEG_BIG, F32)
    l_ref[...] = jnp.zeros(l_ref.shape, F32)
    acc_ref[...] = jnp.zeros(acc_ref.shape, F32)

    def scores(j, s_ref, x_ref):
        r0 = pl.multiple_of(j * tk, tk)
        for mi in range(2):
            cols = slice(mi * dh, (mi + 1) * dh)
            s = lax.dot_general(q_ref[:, cols], k_ref[pl.ds(r0, tk), cols], (((1,), (1,)), ((), ())),
                                preferred_element_type=F32)
            s_ref[mi] = s
            mx = s[:, :LANES]
            for g in range(1, ngrp):
                mx = jnp.maximum(mx, s[:, g * LANES:(g + 1) * LANES])
            x_ref[mi] = mx

    def absorb(j, s_ref, x_ref, masked):
        r0 = pl.multiple_of(j * tk, tk)
        v = v_ref[pl.ds(r0, tk), :]
        for mi in range(2):
            groups = [s_ref[mi, :, g * LANES:(g + 1) * LANES] for g in range(ngrp)]
            if masked:
                row = lax.broadcasted_iota(jnp.int32, (tq, LANES), 0)
                col = lax.broadcasted_iota(jnp.int32, (tq, LANES), 1)
                groups = [jnp.where(col + g * LANES <= row, sg, NEG_BIG) for g, sg in enumerate(groups)]
                mx = functools.reduce(jnp.maximum, groups)
            else:
                mx = x_ref[mi]
            m_prev = m_ref[mi]
            m_new = jnp.maximum(m_prev, jnp.max(mx, axis=1, keepdims=True))
            alpha = jnp.exp2(m_prev - m_new)
            ps = [jnp.exp2(sg - m_new) for sg in groups]
            l_ref[mi] = alpha * l_ref[mi] + functools.reduce(jnp.add, ps)
            p = jnp.concatenate([pg.astype(BF16) for pg in ps], axis=1)
            pv = jnp.dot(p, v, preferred_element_type=F32)
            acc_ref[mi] = jnp.concatenate(
                [acc_ref[mi, :, c * LANES:(c + 1) * LANES] * alpha for c in range(dv // LANES)], axis=1) + pv
            m_ref[mi] = m_new

    scores(0, s_a, x_a)

    def pair(tt, carry):
        j = 2 * tt
        scores(j + 1, s_b, x_b)
        absorb(j, s_a, x_a, False)
        scores(j + 2, s_a, x_a)
        absorb(j + 1, s_b, x_b, False)
        return carry

    lax.fori_loop(0, lax.shift_right_logical(qi, 1), pair, 0)

    @pl.when((qi & 1) == 0)
    def _():
        absorb(qi, s_a, x_a, True)

    @pl.when((qi & 1) == 1)
    def _():
        scores(qi, s_b, x_b)
        absorb(qi - 1, s_a, x_a, False)
        absorb(qi, s_b, x_b, True)

    lam = lam_ref[0, 0]
    l0 = jnp.sum(l_ref[0], axis=1, keepdims=True)
    l1 = jnp.sum(l_ref[1], axis=1, keepdims=True)
    o = acc_ref[0] / l0 - lam * (acc_ref[1] / l1)
    ms = jnp.mean(o * o, axis=-1, keepdims=True)
    o_ref[...] = (o * lax.rsqrt(ms + LN_EPS) * g_ref[...] * out_scale).astype(o_ref.dtype)


def _diff_attention(h, lam, subln_g, *, bsz, seq, out_scale):
    t = h.shape[0]
    dv = subln_g.shape[-1]
    dh = dv // 2
    tq = min(TQ_ATT, seq)
    nq = seq // tq
    assert seq % tq == 0
    hd_n = DIFF_HEADS
    return pl.pallas_call(
        functools.partial(_diff_attn_kernel, dh=dh, out_scale=out_scale),
        grid=(bsz, hd_n, nq),
        in_specs=[
            pl.BlockSpec(memory_space=pltpu.SMEM),
            pl.BlockSpec((tq, dv), lambda b, hd, qi: (b * nq + qi, hd)),
            pl.BlockSpec((seq, dv), lambda b, hd, qi: (b, hd_n + hd)),
            pl.BlockSpec((seq, dv), lambda b, hd, qi: (b, 2 * hd_n + hd)),
            pl.BlockSpec((1, dv), lambda b, hd, qi: (0, 0)),
        ],
        out_specs=pl.BlockSpec((tq, dv), lambda b, hd, qi: (b * nq + qi, hd)),
        out_shape=jax.ShapeDtypeStruct((t, hd_n * dv), BF16),
        scratch_shapes=[pltpu.VMEM((2, tq, tq), F32), pltpu.VMEM((2, tq, tq), F32),
                        pltpu.VMEM((2, tq, LANES), F32), pltpu.VMEM((2, tq, LANES), F32),
                        pltpu.VMEM((2, tq, LANES), F32), pltpu.VMEM((2, tq, LANES), F32),
                        pltpu.VMEM((2, tq, dv), F32)],
        compiler_params=_cparams(3),
        name="diff_attn",
    )(lam.reshape(1, 1).astype(F32), h, h, h, subln_g.reshape(1, dv).astype(F32))


def _conformer_kernel(a_ref, g_ref, ah_ref, gh_ref, w_ref, cb_ref, lg_ref, lb_ref, o_ref, cext_ref, conv_ref,
                      *, tiles_per_seq):
    i = pl.program_id(0)
    tr, width = a_ref.shape
    halo = ah_ref.shape[0]
    ksz = w_ref.shape[0]
    first = (i % tiles_per_seq) == 0

    glu_h = ah_ref[...].astype(F32) * jax.nn.sigmoid(gh_ref[...].astype(F32))
    cext_ref[0:halo, :] = jnp.where(first, 0.0, glu_h)
    cext_ref[halo:halo + tr, :] = a_ref[...].astype(F32) * jax.nn.sigmoid(g_ref[...].astype(F32))
    cext_ref[halo + tr:halo + tr + SUBLANES, :] = jnp.zeros((SUBLANES, width), F32)

    base = halo - (ksz - 1)
    win = CONV_RB + halo + SUBLANES

    def chunk(rc, carry):
        r0 = pl.multiple_of(rc * CONV_RB, CONV_RB)
        for c in range(width // LANES):
            lanes = slice(c * LANES, (c + 1) * LANES)
            wnd = cext_ref[pl.ds(r0, win), lanes]
            acc = jnp.zeros((CONV_RB, LANES), F32)
            for b in range(SUBLANES):
                shifted = wnd if b == 0 else pltpu.roll(wnd, win - b, axis=0)
                for a in range((halo + SUBLANES) // SUBLANES):
                    j = SUBLANES * a + b - base
                    if 0 <= j < ksz:
                        acc = acc + w_ref[j:j + 1, lanes] * shifted[SUBLANES * a:SUBLANES * a + CONV_RB]
            conv_ref[pl.ds(r0, CONV_RB), lanes] = acc + cb_ref[:, lanes]
        return carry

    lax.fori_loop(0, tr // CONV_RB, chunk, 0)
    y = _layer_norm(conv_ref[...], lg_ref[...], lb_ref[...])
    o_ref[...] = (y * jax.nn.sigmoid(y)).astype(o_ref.dtype)


def _conformer(h, conv_w, conv_b, ln_g, ln_b, *, seq, col_a, col_g):
    t = h.shape[0]
    ksz, width = conv_w.shape
    tr = min(TR_MIX, seq)
    halo = CONV_HALO
    assert seq % tr == 0 and tr % halo == 0 and ksz - 1 <= halo and tr % CONV_RB == 0
    rpb = tr // halo
    row = lambda v: v.reshape(1, width).astype(F32)
    return pl.pallas_call(
        functools.partial(_conformer_kernel, tiles_per_seq=seq // tr),
        grid=(t // tr,),
        in_specs=[
            pl.BlockSpec((tr, width), lambda i: (i, col_a)),
            pl.BlockSpec((tr, width), lambda i: (i, col_g)),
            pl.BlockSpec((halo, width), lambda i: (jnp.maximum(i * rpb - 1, 0), col_a)),
            pl.BlockSpec((halo, width), lambda i: (jnp.maximum(i * rpb - 1, 0), col_g)),
            pl.BlockSpec((ksz, width), lambda i: (0, 0)),
            pl.BlockSpec((1, width), lambda i: (0, 0)),
            pl.BlockSpec((1, width), lambda i: (0, 0)),
            pl.BlockSpec((1, width), lambda i: (0, 0)),
        ],
        out_specs=pl.BlockSpec((tr, width), lambda i: (i, 0)),
        out_shape=jax.ShapeDtypeStruct((t, width), BF16),
        scratch_shapes=[pltpu.VMEM((halo + tr + SUBLANES, width), F32), pltpu.VMEM((tr, width), F32)],
        compiler_params=_cparams(1),
        name="conformer_conv",
    )(h, h, h, h, conv_w.astype(F32), row(conv_b), row(ln_g), row(ln_b))


def _gelu_exact(x):
    return 0.5 * x * (1.0 + lax.erf(x * math.sqrt(0.5)))


def _sgu_kernel(u_ref, v_ref, gb_ref, gc_ref, xi_ref, gch_ref, xih_ref, lg_ref, lb_ref, sw_ref, sb_ref, cw_ref,
                o_ref, pext_ref, *, tiles_per_seq):
    i = pl.program_id(0)
    tr, width = u_ref.shape
    halo = gch_ref.shape[0]
    n_grp, chunk, _ = sw_ref.shape
    gdim = width // n_grp
    first = (i % tiles_per_seq) == 0

    vg = _layer_norm(_gelu_exact(v_ref[...].astype(F32)), lg_ref[...], lb_ref[...]).astype(BF16)
    trow = lax.broadcasted_iota(jnp.int32, (chunk, chunk), 0)
    tcol = lax.broadcasted_iota(jnp.int32, (chunk, chunk), 1)
    for g in range(n_grp):
        wg = jnp.where(tcol <= trow, sw_ref[g], 0.0).astype(BF16)
        for n in range(tr // chunk):
            rows = slice(n * chunk, (n + 1) * chunk)
            cols = slice(g * gdim, (g + 1) * gdim)
            sv = jnp.dot(wg, vg[rows, cols], preferred_element_type=F32) + sb_ref[g]
            o_ref[rows, cols] = (_gelu_exact(u_ref[rows, cols].astype(F32)) * sv).astype(o_ref.dtype)

    ph = gch_ref[...].astype(F32) * xih_ref[...].astype(F32)
    pext_ref[0:halo, :] = jnp.where(first, 0.0, ph)
    pext_ref[halo:halo + tr, :] = gc_ref[...].astype(F32) * xi_ref[...].astype(F32)
    pe = pext_ref[...]
    n_ext = halo + tr
    ksz = cw_ref.shape[0]
    conv = cw_ref[ksz - 1:ksz, :] * pe[halo:]
    for back in range(1, ksz):
        conv = conv + cw_ref[ksz - 1 - back:ksz - back, :] * pltpu.roll(pe, back, axis=0)[halo:]
    del n_ext
    o_ref[:, width:2 * width] = (gb_ref[...].astype(F32) * conv).astype(o_ref.dtype)


def _sgu_shortconv(h, ln_g, ln_b, sgu_w, sgu_b, sc_w, *, seq):
    t = h.shape[0]
    width = ln_g.shape[-1]
    n_grp, chunk, _ = sgu_w.shape
    gdim = width // n_grp
    tr = min(TR_MIX, seq)
    halo = SC_HALO
    assert seq % tr == 0 and tr % chunk == 0 and tr % halo == 0 and sc_w.shape[0] - 1 <= halo
    rpb = tr // halo
    row = lambda v: v.reshape(1, width).astype(F32)
    bias = jnp.broadcast_to(sgu_b.astype(F32)[:, :, None], (n_grp, chunk, gdim))
    cur = lambda c: pl.BlockSpec((tr, width), lambda i: (i, c))
    prev = lambda c: pl.BlockSpec((halo, width), lambda i: (jnp.maximum(i * rpb - 1, 0), c))
    whole = lambda a: pl.BlockSpec(a.shape, lambda i: (0,) * a.ndim)
    small = [row(ln_g), row(ln_b), sgu_w.astype(F32), bias, sc_w.astype(F32)]
    return pl.pallas_call(
        functools.partial(_sgu_kernel, tiles_per_seq=seq // tr),
        grid=(t // tr,),
        in_specs=[cur(0), cur(1), cur(2), cur(3), cur(4), prev(3), prev(4)] + [whole(a) for a in small],
        out_specs=pl.BlockSpec((tr, 2 * width), lambda i: (i, 0)),
        out_shape=jax.ShapeDtypeStruct((t, 2 * width), BF16),
        scratch_shapes=[pltpu.VMEM((halo + tr, width), F32)],
        compiler_params=_cparams(1),
        name="sgu_shortconv",
    )(h, h, h, h, h, h, h, *small)


def _outproj_ln_kernel(*refs, n_parts, alpha):
    parts = refs[:n_parts]
    ws = refs[n_parts:2 * n_parts]
    x_ref, g_ref, b_ref, o_ref = refs[2 * n_parts:]
    acc = alpha * x_ref[...]
    for p_ref, w_ref in zip(parts, ws):
        acc = acc + jnp.dot(p_ref[...], w_ref[...], preferred_element_type=F32)
    o_ref[...] = _layer_norm(acc, g_ref[...], b_ref[...])


def _outproj_ln(parts, w_bf16, x, g, b, *, alpha):
    t, d = x.shape
    tm = min(TM_PROJ, t)
    assert t % tm == 0
    in_specs, ws, off = [], [], 0
    for p in parts:
        kp = p.shape[1]
        in_specs.append(pl.BlockSpec((tm, kp), lambda i: (i, 0)))
        ws.append(w_bf16[off:off + kp])
        off += kp
    assert off == w_bf16.shape[0]
    in_specs += [pl.BlockSpec(w.shape, lambda i: (0, 0)) for w in ws]
    in_specs += [pl.BlockSpec((tm, d), lambda i: (i, 0)),
                 pl.BlockSpec((1, d), lambda i: (0, 0)), pl.BlockSpec((1, d), lambda i: (0, 0))]
    return pl.pallas_call(
        functools.partial(_outproj_ln_kernel, n_parts=len(parts), alpha=alpha),
        grid=(t // tm,),
        in_specs=in_specs,
        out_specs=pl.BlockSpec((tm, d), lambda i: (i, 0)),
        out_shape=jax.ShapeDtypeStruct((t, d), F32),
        compiler_params=_cparams(1),
        name="outproj_ln",
    )(*parts, *ws, x, g.reshape(1, d).astype(F32), b.reshape(1, d).astype(F32))


def _cross_router_kernel(x_ref, wq_ref, k_ref, v_ref, wo_ref, g_ref, b_ref, wr_ref, br_ref, o_ref, r_ref,
                         *, alpha, scale, n_heads, n_groups, per_group):
    x = x_ref[...]
    q = (jnp.dot(x.astype(BF16), wq_ref[...], preferred_element_type=F32) * scale).astype(BF16)
    dh = q.shape[1] // n_heads
    outs = []
    for hh in range(n_heads):
        cols = slice(hh * dh, (hh + 1) * dh)
        s = lax.dot_general(q[:, cols], k_ref[:, cols], (((1,), (1,)), ((), ())), preferred_element_type=F32)
        e = jnp.exp(s - jnp.max(s, axis=-1, keepdims=True))
        pr = e / jnp.sum(e, axis=-1, keepdims=True)
        outs.append(jnp.dot(pr.astype(BF16), v_ref[:, cols], preferred_element_type=F32))
    o = jnp.concatenate(outs, axis=-1).astype(BF16)
    y = _layer_norm(alpha * x + jnp.dot(o, wo_ref[...], preferred_element_type=F32), g_ref[...], b_ref[...])
    o_ref[...] = y

    logits = jnp.dot(y.astype(BF16), wr_ref[...], preferred_element_type=F32) + br_ref[...]
    lane = lax.broadcasted_iota(jnp.int32, logits.shape, 1)
    far = jnp.int32(LANES)
    is_g = lane < n_groups
    gl = jnp.where(is_g, logits, -jnp.inf)
    gmax = jnp.max(gl, axis=-1, keepdims=True)
    grp = jnp.min(jnp.where(is_g & (gl == gmax), lane, far), axis=-1, keepdims=True)
    g_gate = 1.0 / jnp.sum(jnp.where(is_g, jnp.exp(gl - gmax), 0.0), axis=-1, keepdims=True)
    lo = n_groups + per_group * grp
    is_e = (lane >= lo) & (lane < lo + per_group)
    el = jnp.where(is_e, logits, -jnp.inf)
    v1 = jnp.max(el, axis=-1, keepdims=True)
    i1 = jnp.min(jnp.where(is_e & (el == v1), lane, far), axis=-1, keepdims=True)
    is_e2 = is_e & (lane != i1)
    el2 = jnp.where(is_e2, logits, -jnp.inf)
    v2 = jnp.max(el2, axis=-1, keepdims=True)
    i2 = jnp.min(jnp.where(is_e2 & (el2 == v2), lane, far), axis=-1, keepdims=True)
    e2 = jnp.exp(v2 - v1)
    w1 = g_gate / (1.0 + e2)
    w2 = g_gate * e2 / (1.0 + e2)
    r = jnp.where(lane == 0, (i1 - n_groups).astype(F32),
                  jnp.where(lane == 1, (i2 - n_groups).astype(F32),
                            jnp.where(lane == 2, w1, jnp.where(lane == 3, w2, 0.0))))
    r_ref[...] = r


def _cross_router(x, kv_bf16, wq_bf16, wo_bf16, g, b, wr_bf16, br, *, seq, mem_len, alpha, per_group):
    t, d = x.shape
    cw = wq_bf16.shape[1]
    tm = min(TM_PROJ, seq)
    assert seq % tm == 0
    tps = seq // tm
    dh = cw // CROSS_HEADS
    n_exp = wr_bf16.shape[1]
    return pl.pallas_call(
        functools.partial(_cross_router_kernel, alpha=alpha, scale=dh ** -0.5, n_heads=CROSS_HEADS,
                          n_groups=N_GROUPS, per_group=per_group),
        grid=(t // tm,),
        in_specs=[
            pl.BlockSpec((tm, d), lambda i: (i, 0)),
            pl.BlockSpec((d, cw), lambda i: (0, 0)),
            pl.BlockSpec((mem_len, cw), lambda i: (i // tps, 0)),
            pl.BlockSpec((mem_len, cw), lambda i: (i // tps, 1)),
            pl.BlockSpec((cw, d), lambda i: (0, 0)),
            pl.BlockSpec((1, d), lambda i: (0, 0)),
            pl.BlockSpec((1, d), lambda i: (0, 0)),
            pl.BlockSpec((d, n_exp), lambda i: (0, 0)),
            pl.BlockSpec((1, n_exp), lambda i: (0, 0)),
        ],
        out_specs=[pl.BlockSpec((tm, d), lambda i: (i, 0)), pl.BlockSpec((tm, LANES), lambda i: (i, 0))],
        out_shape=[jax.ShapeDtypeStruct((t, d), F32), jax.ShapeDtypeStruct((t, LANES), F32)],
        compiler_params=_cparams(1),
        name="cross_attn_router",
    )(x, wq_bf16, kv_bf16, kv_bf16, wo_bf16, g.reshape(1, d).astype(F32), b.reshape(1, d).astype(F32),
      wr_bf16, br)


def _moe_kernel(be_ref, nu_ref, tok0_ref, tokn_ref, dstp_ref, dstc_ref, gw_ref, w1_ref, w3_ref, w2_ref,
                x_hbm, o_hbm, xb0, xb1, yb0, yb1, wb1, wb3, wb2, gsem, ssem):
    i = pl.program_id(0)
    n_used = nu_ref[0]
    tb = xb0.shape[0]
    xbufs, ybufs = (xb0, xb1), (yb0, yb1)

    def gather_row(tok, r, s):
        return pltpu.make_async_copy(x_hbm.at[pl.ds(tok, 1), :], xbufs[s].at[pl.ds(r, 1), :], gsem.at[s])

    def scatter_row(dst, r, s):
        return pltpu.make_async_copy(ybufs[s].at[pl.ds(r, 1), :], o_hbm.at[pl.ds(dst, 1), :], ssem.at[s])

    def wait_gather(s):
        pltpu.make_async_copy(x_hbm.at[pl.ds(0, tb), :], xbufs[s], gsem.at[s]).wait()

    def wait_scatter(s):
        pltpu.make_async_copy(ybufs[s], o_hbm.at[pl.ds(0, tb), :], ssem.at[s]).wait()

    def step(slot):
        other = 1 - slot

        @pl.when(i == 0)
        def _():
            def body(r, c):
                gather_row(tok0_ref[0, 0, r], r, 0).start()
                return c
            lax.fori_loop(0, tb, body, 0, unroll=8)
            yb1[...] = jnp.zeros(yb1.shape, F32)

        first_of_expert = jnp.logical_or(i == 0, be_ref[i] != be_ref[jnp.maximum(i - 1, 0)])

        @pl.when(first_of_expert)
        def _():
            wb1[...] = w1_ref[0].astype(BF16)
            wb3[...] = w3_ref[0].astype(BF16)
            wb2[...] = w2_ref[0].astype(BF16)

        wait_gather(slot)
        for r in range(tb):
            gather_row(tokn_ref[0, 0, r], r, other).start()
        for r in range(tb):
            scatter_row(dstp_ref[0, 0, r], r, other).start()
        xb = xbufs[slot][...].astype(BF16)
        h1 = jnp.dot(xb, wb1[...], preferred_element_type=F32)
        h3 = jnp.dot(xb, wb3[...], preferred_element_type=F32)
        hb = (h1 * jax.nn.sigmoid(h1) * h3).astype(BF16)
        ybufs[slot][...] = jnp.dot(hb, wb2[...], preferred_element_type=F32) * gw_ref[...]
        wait_scatter(other)

        @pl.when(i == n_used - 1)
        def _():
            def body(r, c):
                scatter_row(dstc_ref[0, 0, r], r, slot).start()
                return c
            lax.fori_loop(0, tb, body, 0, unroll=8)
            wait_scatter(slot)
            wait_gather(other)

    @pl.when(jnp.logical_and(i < n_used, (i & 1) == 0))
    def _():
        step(0)

    @pl.when(jnp.logical_and(i < n_used, (i & 1) == 1))
    def _():
        step(1)


def _moe_experts(x, blk_e, n_used, tok, dst, gw, w1, w3, w2, *, n_out_rows):
    t, d = x.shape
    nb = blk_e.shape[0]
    tb = tok.shape[0] // nb
    ff = w1.shape[-1]
    tok3 = tok.reshape(nb, 1, tb)
    dst3 = dst.reshape(nb, 1, tb)
    smem_blk = lambda f: pl.BlockSpec((1, 1, tb), f, memory_space=pltpu.SMEM)
    grid_spec = pltpu.PrefetchScalarGridSpec(
        num_scalar_prefetch=2,
        grid=(nb,),
        in_specs=[
            smem_blk(lambda i, be, nu: (0, 0, 0)),
            smem_blk(lambda i, be, nu: (jnp.minimum(i + 1, nb - 1), 0, 0)),
            smem_blk(lambda i, be, nu: (jnp.where(i == 0, nb - 1, i - 1), 0, 0)),
            smem_blk(lambda i, be, nu: (i, 0, 0)),
            pl.BlockSpec((tb, 1), lambda i, be, nu: (i, 0)),
            pl.BlockSpec((1, d, ff), lambda i, be, nu: (be[i], 0, 0)),
            pl.BlockSpec((1, d, ff), lambda i, be, nu: (be[i], 0, 0)),
            pl.BlockSpec((1, ff, d), lambda i, be, nu: (be[i], 0, 0)),
            pl.BlockSpec(memory_space=pl.ANY),
        ],
        out_specs=pl.BlockSpec(memory_space=pl.ANY),
        scratch_shapes=[pltpu.VMEM((tb, d), F32), pltpu.VMEM((tb, d), F32),
                        pltpu.VMEM((tb, d), F32), pltpu.VMEM((tb, d), F32),
                        pltpu.VMEM((d, ff), BF16), pltpu.VMEM((d, ff), BF16), pltpu.VMEM((ff, d), BF16),
                        pltpu.SemaphoreType.DMA((2,)), pltpu.SemaphoreType.DMA((2,))],
    )
    return pl.pallas_call(
        _moe_kernel,
        grid_spec=grid_spec,
        out_shape=jax.ShapeDtypeStruct((n_out_rows, d), F32),
        compiler_params=_cparams(1),
        name="moe_experts",
    )(blk_e, n_used, tok3, tok3, dst3, dst3, gw, w1, w3, w2, x)


def _moe_dispatch(route, *, n_experts, tb):
    t = route.shape[0]
    n_assign = TOP_K * t
    nb = n_assign // tb + n_experts
    eid = route[:, :TOP_K].astype(jnp.int32).reshape(-1)
    wgt = route[:, TOP_K:2 * TOP_K].reshape(-1)
    assert n_experts * n_assign < 2 ** 31
    order = jnp.sort(eid * n_assign + jnp.arange(n_assign, dtype=jnp.int32)) % n_assign
    counts = jnp.sum((eid[:, None] == jnp.arange(n_experts, dtype=jnp.int32)[None, :]).astype(jnp.int32), axis=0)
    blocks = (counts + tb - 1) // tb
    bend = jnp.cumsum(blocks)
    start = jnp.cumsum(counts) - counts
    blk = jnp.arange(nb, dtype=jnp.int32)
    blk_e = jnp.minimum(jnp.sum((blk[:, None] >= bend[None, :]).astype(jnp.int32), axis=1), n_experts - 1)
    k_in_e = blk - (bend - blocks)[blk_e]
    n_valid = jnp.clip(counts[blk_e] - k_in_e * tb, 0, tb)
    r = jnp.arange(tb, dtype=jnp.int32)
    valid = r[None, :] < n_valid[:, None]
    src = jnp.where(valid, (start[blk_e] + k_in_e * tb)[:, None] + r[None, :], 0)
    a = order[src]
    a_tok = a // TOP_K
    tok = jnp.where(valid, a_tok, 0).reshape(-1)
    dst = jnp.where(valid, (a % TOP_K) * t + a_tok, TOP_K * t + r[None, :]).reshape(-1)
    gw = jnp.where(valid, wgt[a], 0.0).reshape(-1, 1)
    return blk_e.astype(jnp.int32), bend[-1:].astype(jnp.int32), tok, dst, gw


def _combine_ln_kernel(x_ref, y0_ref, y1_ref, g_ref, b_ref, o_ref, *, alpha):
    o_ref[...] = _layer_norm(alpha * x_ref[...] + y0_ref[...] + y1_ref[...], g_ref[...], b_ref[...])


def _combine_ln(x, y, g, b, *, alpha):
    t, d = x.shape
    tm = min(TM_PROJ, t)
    assert t % tm == 0
    nt = t // tm
    return pl.pallas_call(
        functools.partial(_combine_ln_kernel, alpha=alpha),
        grid=(nt,),
        in_specs=[pl.BlockSpec((tm, d), lambda i: (i, 0)),
                  pl.BlockSpec((tm, d), lambda i: (i, 0)),
                  pl.BlockSpec((tm, d), lambda i: (i + nt, 0)),
                  pl.BlockSpec((1, d), lambda i: (0, 0)), pl.BlockSpec((1, d), lambda i: (0, 0))],
        out_specs=pl.BlockSpec((tm, d), lambda i: (i, 0)),
        out_shape=jax.ShapeDtypeStruct((t, d), F32),
        compiler_params=_cparams(1),
        name="moe_combine_ln",
    )(x, y, y, g.reshape(1, d).astype(F32), b.reshape(1, d).astype(F32))


def _rope_tables(positions, dh):
    rope_dim = dh // 4
    half = rope_dim // 2
    assert half == ROPE_HALF and dh == LANES
    inv_freq = ROPE_THETA ** (-jnp.arange(0, rope_dim, 2, dtype=F32) / rope_dim)
    ang = positions.astype(F32).reshape(-1, 1) * inv_freq[None, :]
    cos, sin = jnp.cos(ang), jnp.sin(ang)
    t = ang.shape[0]
    c = jnp.concatenate([cos, cos, jnp.ones((t, dh - rope_dim), F32)], axis=1)
    s1 = jnp.concatenate([-sin, jnp.zeros((t, dh - half), F32)], axis=1)
    s2 = jnp.concatenate([jnp.zeros((t, half), F32), sin, jnp.zeros((t, dh - rope_dim), F32)], axis=1)
    return c, s1, s2


def kernel(x, mem, positions, w_in, w_out, ln_mix_g, ln_mix_b, ln_mem_g, ln_mem_b, ln_ffn_g, ln_ffn_b, lam_q1, lam_k1, lam_q2, lam_k2, diff_subln_g, conv_w, conv_b, conv_ln_g, conv_ln_b, sgu_ln_g, sgu_ln_b, sgu_w, sgu_b, sc_w, mem_kv_w, xq_w, xo_w, rg_w, rg_b, re_w, re_b, e_w1, e_w3, e_w2):
    bsz, seq, d = x.shape
    t = bsz * seq
    depth = w_in.shape[0]
    mem_len = mem.shape[1]
    half = w_in.shape[2] // 5
    dh = lam_q1.shape[-1]
    n_experts = re_w.shape[-1]
    alpha = (2 * depth) ** 0.25
    assert half % TN_PROJ == 0 and diff_subln_g.shape[-1] == 2 * dh and half == 2 * DIFF_HEADS * dh

    xt = x.reshape(t, d)
    rope = _rope_tables(positions, dh)
    kv = _inproj(mem.reshape(bsz * mem_len, d), mem_kv_w.astype(BF16))

    for l in range(depth):
        j = l // 2
        w_l = w_in[l].astype(BF16)
        if l % 2 == 0:
            h = _inproj(xt, w_l, rope, n_rope_tiles=2 * half // TN_PROJ, n_scale_tiles=half // TN_PROJ,
                        scale=dh ** -0.5 * math.log2(math.e))
            lam_init = 0.8 - 0.6 * math.exp(-0.3 * l)
            lam = (jnp.exp(jnp.sum(lam_q1[j].astype(F32) * lam_k1[j].astype(F32)))
                   - jnp.exp(jnp.sum(lam_q2[j].astype(F32) * lam_k2[j].astype(F32))) + lam_init)
            o = _diff_attention(h, lam, diff_subln_g[j], bsz=bsz, seq=seq, out_scale=1.0 - lam_init)
            c = _conformer(h, conv_w[j], conv_b[j], conv_ln_g[j], conv_ln_b[j], seq=seq, col_a=3, col_g=4)
            parts = [o, c]
        else:
            h = _inproj(xt, w_l)
            parts = [_sgu_shortconv(h, sgu_ln_g[j], sgu_ln_b[j], sgu_w[j], sgu_b[j], sc_w[j], seq=seq)]
        xt = _outproj_ln(parts, w_out[l].astype(BF16), xt, ln_mix_g[l], ln_mix_b[l], alpha=alpha)

        n_route = N_GROUPS + n_experts
        wr = jnp.concatenate([rg_w[l], re_w[l], jnp.zeros((d, LANES - n_route), F32)], axis=1).astype(BF16)
        br = jnp.concatenate([rg_b[l], re_b[l], jnp.zeros((LANES - n_route,), F32)]).reshape(1, LANES).astype(F32)
        xt, route = _cross_router(xt, kv, xq_w[l].astype(BF16), xo_w[l].astype(BF16), ln_mem_g[l], ln_mem_b[l],
                                  wr, br, seq=seq, mem_len=mem_len, alpha=alpha, per_group=n_experts // N_GROUPS)

        blk_e, n_used, tok, dst, gw = _moe_dispatch(route, n_experts=n_experts, tb=TB_MOE)
        y = _moe_experts(xt, blk_e, n_used, tok, dst, gw, e_w1[l], e_w3[l], e_w2[l], n_out_rows=TOP_K * t + TB_MOE)
        xt = _combine_ln(xt, y, ln_ffn_g[l], ln_ffn_b[l], alpha=alpha)
    return xt.reshape(bsz, seq, d)
```

```python
import functools
import math

import jax
import jax.numpy as jnp
from jax import lax
from jax.experimental import pallas as pl
from jax.experimental.pallas import tpu as pltpu

F32 = jnp.float32
BF16 = jnp.bfloat16

DIFF_HEADS = 4
ROPE_THETA = 500000.0
CONF_KERNEL = 31
SGU_CHUNK = 128
CROSS_HEADS = 4
N_GROUPS = 4
TOP_K = 2
LN_EPS = 1e-5

LANES = 128
SUBLANES = 8
NEG_BIG = -1e30
VMEM_LIMIT = 56 * 1024 * 1024

TM_PROJ = 512
TM_INPROJ = 1024
TN_PROJ = 1024
TQ_ATT = 512
ATT_RC = 64
TR_MIX = 256
CONV_HALO = 32
CONV_RB = 32
SC_HALO = 16
TB_MOE = 256


def _cparams(n_axes):
    return pltpu.CompilerParams(dimension_semantics=("arbitrary",) * n_axes,
                                vmem_limit_bytes=VMEM_LIMIT)


def _layer_norm(y, g, b):
    mu = jnp.mean(y, axis=-1, keepdims=True)
    d = y - mu
    var = jnp.mean(d * d, axis=-1, keepdims=True)
    return d * lax.rsqrt(var + LN_EPS) * g + b


def _inproj_kernel(x_ref, w_ref, *rest, n_rope_tiles, n_scale_tiles, scale):
    if n_rope_tiles:
        c_ref, s1_ref, s2_ref, o_ref, xb_ref = rest
    else:
        o_ref, xb_ref = rest
    j = pl.program_id(1)

    @pl.when(j == 0)
    def _():
        xb_ref[...] = x_ref[...].astype(BF16)

    acc = jnp.dot(xb_ref[...], w_ref[...], preferred_element_type=F32)
    if not n_rope_tiles:
        o_ref[...] = acc.astype(o_ref.dtype)
        return

    @pl.when(j >= n_rope_tiles)
    def _():
        o_ref[...] = acc.astype(o_ref.dtype)

    @pl.when(j < n_rope_tiles)
    def _():
        a = acc * jnp.where(j < n_scale_tiles, scale, 1.0).astype(F32)
        c, s1, s2 = c_ref[...], s1_ref[...], s2_ref[...]
        for g in range(a.shape[1] // LANES):
            ag = a[:, g * LANES:(g + 1) * LANES]
            og = (ag * c + pltpu.roll(ag, LANES - ROPE_HALF, axis=1) * s1
                  + pltpu.roll(ag, ROPE_HALF, axis=1) * s2)
            o_ref[:, g * LANES:(g + 1) * LANES] = og.astype(o_ref.dtype)


ROPE_HALF = 16


def _inproj(x, w_bf16, rope=None, *, n_rope_tiles=0, n_scale_tiles=0, scale=1.0, out_dtype=BF16):
    m, k = x.shape
    n = w_bf16.shape[1]
    tm = min(TM_INPROJ, m)
    tn = min(TN_PROJ, n)
    assert m % tm == 0 and n % tn == 0
    in_specs = [pl.BlockSpec((tm, k), lambda i, j: (i, 0)),
                pl.BlockSpec((k, tn), lambda i, j: (0, j))]
    args = [x, w_bf16]
    if n_rope_tiles:
        in_specs += [pl.BlockSpec((tm, LANES), lambda i, j: (i, 0))] * 3
        args += list(rope)
    return pl.pallas_call(
        functools.partial(_inproj_kernel, n_rope_tiles=n_rope_tiles, n_scale_tiles=n_scale_tiles, scale=scale),
        grid=(m // tm, n // tn),
        in_specs=in_specs,
        out_specs=pl.BlockSpec((tm, tn), lambda i, j: (i, j)),
        out_shape=jax.ShapeDtypeStruct((m, n), out_dtype),
        scratch_shapes=[pltpu.VMEM((tm, k), BF16)],
        compiler_params=_cparams(2),
        name="inproj_rope" if n_rope_tiles else "inproj",
    )(*args)


def _diff_attn_kernel(lam_ref, q_ref, k_ref, v_ref, g_ref, o_ref, s_a, s_b, x_a, x_b, p_a, p_b, a_a, a_b,
                      m_ref, l_ref, acc_ref, *, dh, out_scale):
    qi = pl.program_id(2)
    tq = q_ref.shape[0]
    tk = tq
    dv = v_ref.shape[1]
    ngrp = tk // LANES
    m_ref[...] = jnp.full(m_ref.shape, NEG_BIG, F32)
    l_ref[...] = jnp.zeros(l_ref.shape, F32)
    acc_ref[...] = jnp.zeros(acc_ref.shape, F32)

    def scores(j, s_ref, x_ref):
        r0 = pl.multiple_of(j * tk, tk)
        for mi in range(2):
            cols = slice(mi * dh, (mi + 1) * dh)
            s = lax.dot_general(q_ref[:, cols], k_ref[pl.ds(r0, tk), cols], (((1,), (1,)), ((), ())),
                                preferred_element_type=F32)
            s_ref[mi] = s
            mx = s[:, :LANES]
            for g in range(1, ngrp):
                mx = jnp.maximum(mx, s[:, g * LANES:(g + 1) * LANES])
            x_ref[mi] = mx

    def softmax(s_ref, x_ref, p_ref, a_ref, masked):
        for mi in range(2):
            for rc in range(tq // ATT_RC):
                rows = slice(rc * ATT_RC, (rc + 1) * ATT_RC)
                groups = [s_ref[mi, rows, g * LANES:(g + 1) * LANES] for g in range(ngrp)]
                if masked:
                    row = lax.broadcasted_iota(jnp.int32, (ATT_RC, LANES), 0) + rc * ATT_RC
                    col = lax.broadcasted_iota(jnp.int32, (ATT_RC, LANES), 1)
                    groups = [jnp.where(col + g * LANES <= row, sg, NEG_BIG) for g, sg in enumerate(groups)]
                    mx = functools.reduce(jnp.maximum, groups)
                else:
                    mx = x_ref[mi, rows]
                m_prev = m_ref[mi, rows]
                m_new = jnp.maximum(m_prev, jnp.max(mx, axis=1, keepdims=True))
                alpha = jnp.exp2(m_prev - m_new)
                lsum = None
                for g, sg in enumerate(groups):
                    pg = jnp.exp2(sg - m_new)
                    lsum = pg if lsum is None else lsum + pg
                    p_ref[mi, rows, g * LANES:(g + 1) * LANES] = pg.astype(BF16)
                l_ref[mi, rows] = alpha * l_ref[mi, rows] + lsum
                m_ref[mi, rows] = m_new
                a_ref[mi, rows] = alpha

    def weighted_values(j, p_ref, a_ref):
        r0 = pl.multiple_of(j * tk, tk)
        v = v_ref[pl.ds(r0, tk), :]
        for mi in range(2):
            pv = jnp.dot(p_ref[mi], v, preferred_element_type=F32)
            alpha = a_ref[mi]
            acc_ref[mi] = jnp.concatenate(
                [acc_ref[mi, :, c * LANES:(c + 1) * LANES] * alpha for c in range(dv // LANES)], axis=1) + pv

    scores(0, s_a, x_a)

    def pair(u, carry):
        j = 2 * u
        scores(j + 1, s_b, x_b)
        softmax(s_a, x_a, p_a, a_a, False)
        weighted_values(j, p_a, a_a)
        scores(j + 2, s_a, x_a)
        softmax(s_b, x_b, p_b, a_b, False)
        weighted_values(j + 1, p_b, a_b)
        return carry

    lax.fori_loop(0, lax.shift_right_logical(qi, 1), pair, 0)

    @pl.when((qi & 1) == 0)
    def _():
        softmax(s_a, x_a, p_a, a_a, True)
        weighted_values(qi, p_a, a_a)

    @pl.when((qi & 1) == 1)
    def _():
        scores(qi, s_b, x_b)
        softmax(s_a, x_a, p_a, a_a, False)
        weighted_values(qi - 1, p_a, a_a)
        softmax(s_b, x_b, p_b, a_b, True)
        weighted_values(qi, p_b, a_b)

    lam = lam_ref[0, 0]
    l0 = jnp.sum(l_ref[0], axis=1, keepdims=True)
    l1 = jnp.sum(l_ref[1], axis=1, keepdims=True)
    o = acc_ref[0] / l0 - lam * (acc_ref[1] / l1)
    ms = jnp.mean(o * o, axis=-1, keepdims=True)
    o_ref[...] = (o * lax.rsqrt(ms + LN_EPS) * g_ref[...] * out_scale).astype(o_ref.dtype)


def _diff_attention(h, lam, subln_g, *, bsz, seq, out_scale):
    t = h.shape[0]
    dv = subln_g.shape[-1]
    dh = dv // 2
    tq = min(TQ_ATT, seq)
    nq = seq // tq
    assert seq % tq == 0
    hd_n = DIFF_HEADS
    return pl.pallas_call(
        functools.partial(_diff_attn_kernel, dh=dh, out_scale=out_scale),
        grid=(bsz, hd_n, nq),
        in_specs=[
            pl.BlockSpec(memory_space=pltpu.SMEM),
            pl.BlockSpec((tq, dv), lambda b, hd, qi: (b * nq + qi, hd)),
            pl.BlockSpec((seq, dv), lambda b, hd, qi: (b, hd_n + hd)),
            pl.BlockSpec((seq, dv), lambda b, hd, qi: (b, 2 * hd_n + hd)),
            pl.BlockSpec((1, dv), lambda b, hd, qi: (0, 0)),
        ],
        out_specs=pl.BlockSpec((tq, dv), lambda b, hd, qi: (b * nq + qi, hd)),
        out_shape=jax.ShapeDtypeStruct((t, hd_n * dv), BF16),
        scratch_shapes=[pltpu.VMEM((2, tq, tq), F32), pltpu.VMEM((2, tq, tq), F32),
                        pltpu.VMEM((2, tq, LANES), F32), pltpu.VMEM((2, tq, LANES), F32),
                        pltpu.VMEM((2, tq, tq), BF16), pltpu.VMEM((2, tq, tq), BF16),
                        pltpu.VMEM((2, tq, LANES), F32), pltpu.VMEM((2, tq, LANES), F32),
                        pltpu.VMEM((2, tq, LANES), F32), pltpu.VMEM((2, tq, LANES), F32),
                        pltpu.VMEM((2, tq, dv), F32)],
        compiler_params=_cparams(3),
        name="diff_attn",
    )(lam.reshape(1, 1).astype(F32), h, h, h, subln_g.reshape(1, dv).astype(F32))


def _conformer_kernel(a_ref, g_ref, ah_ref, gh_ref, w_ref, cb_ref, lg_ref, lb_ref, o_ref, cext_ref, conv_ref,
                      *, tiles_per_seq):
    i = pl.program_id(0)
    tr, width = a_ref.shape
    halo = ah_ref.shape[0]
    ksz = w_ref.shape[0]
    first = (i % tiles_per_seq) == 0

    glu_h = ah_ref[...].astype(F32) * jax.nn.sigmoid(gh_ref[...].astype(F32))
    cext_ref[0:halo, :] = jnp.where(first, 0.0, glu_h)
    cext_ref[halo:halo + tr, :] = a_ref[...].astype(F32) * jax.nn.sigmoid(g_ref[...].astype(F32))
    cext_ref[halo + tr:halo + tr + SUBLANES, :] = jnp.zeros((SUBLANES, width), F32)

    base = halo - (ksz - 1)
    win = CONV_RB + halo + SUBLANES

    def chunk(rc, carry):
        r0 = pl.multiple_of(rc * CONV_RB, CONV_RB)
        for c in range(width // LANES):
            lanes = slice(c * LANES, (c + 1) * LANES)
            wnd = cext_ref[pl.ds(r0, win), lanes]
            acc = jnp.zeros((CONV_RB, LANES), F32)
            for b in range(SUBLANES):
                shifted = wnd if b == 0 else pltpu.roll(wnd, win - b, axis=0)
                for a in range((halo + SUBLANES) // SUBLANES):
                    j = SUBLANES * a + b - base
                    if 0 <= j < ksz:
                        acc = acc + w_ref[j:j + 1, lanes] * shifted[SUBLANES * a:SUBLANES * a + CONV_RB]
            conv_ref[pl.ds(r0, CONV_RB), lanes] = acc + cb_ref[:, lanes]
        return carry

    lax.fori_loop(0, tr // CONV_RB, chunk, 0)
    y = _layer_norm(conv_ref[...], lg_ref[...], lb_ref[...])
    o_ref[...] = (y * jax.nn.sigmoid(y)).astype(o_ref.dtype)


def _conformer(h, conv_w, conv_b, ln_g, ln_b, *, seq, col_a, col_g):
    t = h.shape[0]
    ksz, width = conv_w.shape
    tr = min(TR_MIX, seq)
    halo = CONV_HALO
    assert seq % tr == 0 and tr % halo == 0 and ksz - 1 <= halo and tr % CONV_RB == 0
    rpb = tr // halo
    row = lambda v: v.reshape(1, width).astype(F32)
    return pl.pallas_call(
        functools.partial(_conformer_kernel, tiles_per_seq=seq // tr),
        grid=(t // tr,),
        in_specs=[
            pl.BlockSpec((tr, width), lambda i: (i, col_a)),
            pl.BlockSpec((tr, width), lambda i: (i, col_g)),
            pl.BlockSpec((halo, width), lambda i: (jnp.maximum(i * rpb - 1, 0), col_a)),
            pl.BlockSpec((halo, width), lambda i: (jnp.maximum(i * rpb - 1, 0), col_g)),
            pl.BlockSpec((ksz, width), lambda i: (0, 0)),
            pl.BlockSpec((1, width), lambda i: (0, 0)),
            pl.BlockSpec((1, width), lambda i: (0, 0)),
            pl.BlockSpec((1, width), lambda i: (0, 0)),
        ],
        out_specs=pl.BlockSpec((tr, width), lambda i: (i, 0)),
        out_shape=jax.ShapeDtypeStruct((t, width), BF16),
        scratch_shapes=[pltpu.VMEM((halo + tr + SUBLANES, width), F32), pltpu.VMEM((tr, width), F32)],
        compiler_params=_cparams(1),
        name="conformer_conv",
    )(h, h, h, h, conv_w.astype(F32), row(conv_b), row(ln_g), row(ln_b))


def _gelu_exact(x):
    return 0.5 * x * (1.0 + lax.erf(x * math.sqrt(0.5)))


def _sgu_kernel(u_ref, v_ref, gb_ref, gc_ref, xi_ref, gch_ref, xih_ref, lg_ref, lb_ref, sw_ref, sb_ref, cw_ref,
                o_ref, pext_ref, *, tiles_per_seq):
    i = pl.program_id(0)
    tr, width = u_ref.shape
    halo = gch_ref.shape[0]
    n_grp, chunk, _ = sw_ref.shape
    gdim = width // n_grp
    first = (i % tiles_per_seq) == 0

    vg = _layer_norm(_gelu_exact(v_ref[...].astype(F32)), lg_ref[...], lb_ref[...]).astype(BF16)
    trow = lax.broadcasted_iota(jnp.int32, (chunk, chunk), 0)
    tcol = lax.broadcasted_iota(jnp.int32, (chunk, chunk), 1)
    for g in range(n_grp):
        wg = jnp.where(tcol <= trow, sw_ref[g], 0.0).astype(BF16)
        for n in range(tr // chunk):
            rows = slice(n * chunk, (n + 1) * chunk)
            cols = slice(g * gdim, (g + 1) * gdim)
            sv = jnp.dot(wg, vg[rows, cols], preferred_element_type=F32) + sb_ref[g]
            o_ref[rows, cols] = (_gelu_exact(u_ref[rows, cols].astype(F32)) * sv).astype(o_ref.dtype)

    ph = gch_ref[...].astype(F32) * xih_ref[...].astype(F32)
    pext_ref[0:halo, :] = jnp.where(first, 0.0, ph)
    pext_ref[halo:halo + tr, :] = gc_ref[...].astype(F32) * xi_ref[...].astype(F32)
    pe = pext_ref[...]
    n_ext = halo + tr
    ksz = cw_ref.shape[0]
    conv = cw_ref[ksz - 1:ksz, :] * pe[halo:]
    for back in range(1, ksz):
        conv = conv + cw_ref[ksz - 1 - back:ksz - back, :] * pltpu.roll(pe, back, axis=0)[halo:]
    del n_ext
    o_ref[:, width:2 * width] = (gb_ref[...].astype(F32) * conv).astype(o_ref.dtype)


def _sgu_shortconv(h, ln_g, ln_b, sgu_w, sgu_b, sc_w, *, seq):
    t = h.shape[0]
    width = ln_g.shape[-1]
    n_grp, chunk, _ = sgu_w.shape
    gdim = width // n_grp
    tr = min(TR_MIX, seq)
    halo = SC_HALO
    assert seq % tr == 0 and tr % chunk == 0 and tr % halo == 0 and sc_w.shape[0] - 1 <= halo
    rpb = tr // halo
    row = lambda v: v.reshape(1, width).astype(F32)
    bias = jnp.broadcast_to(sgu_b.astype(F32)[:, :, None], (n_grp, chunk, gdim))
    cur = lambda c: pl.BlockSpec((tr, width), lambda i: (i, c))
    prev = lambda c: pl.BlockSpec((halo, width), lambda i: (jnp.maximum(i * rpb - 1, 0), c))
    whole = lambda a: pl.BlockSpec(a.shape, lambda i: (0,) * a.ndim)
    small = [row(ln_g), row(ln_b), sgu_w.astype(F32), bias, sc_w.astype(F32)]
    return pl.pallas_call(
        functools.partial(_sgu_kernel, tiles_per_seq=seq // tr),
        grid=(t // tr,),
        in_specs=[cur(0), cur(1), cur(2), cur(3), cur(4), prev(3), prev(4)] + [whole(a) for a in small],
        out_specs=pl.BlockSpec((tr, 2 * width), lambda i: (i, 0)),
        out_shape=jax.ShapeDtypeStruct((t, 2 * width), BF16),
        scratch_shapes=[pltpu.VMEM((halo + tr, width), F32)],
        compiler_params=_cparams(1),
        name="sgu_shortconv",
    )(h, h, h, h, h, h, h, *small)


def _outproj_ln_kernel(*refs, n_parts, alpha):
    parts = refs[:n_parts]
    ws = refs[n_parts:2 * n_parts]
    x_ref, g_ref, b_ref, o_ref = refs[2 * n_parts:]
    acc = alpha * x_ref[...]
    for p_ref, w_ref in zip(parts, ws):
        acc = acc + jnp.dot(p_ref[...], w_ref[...], preferred_element_type=F32)
    o_ref[...] = _layer_norm(acc, g_ref[...], b_ref[...])


def _outproj_ln(parts, w_bf16, x, g, b, *, alpha):
    t, d = x.shape
    tm = min(TM_PROJ, t)
    assert t % tm == 0
    in_specs, ws, off = [], [], 0
    for p in parts:
        kp = p.shape[1]
        in_specs.append(pl.BlockSpec((tm, kp), lambda i: (i, 0)))
        ws.append(w_bf16[off:off + kp])
        off += kp
    assert off == w_bf16.shape[0]
    in_specs += [pl.BlockSpec(w.shape, lambda i: (0, 0)) for w in ws]
    in_specs += [pl.BlockSpec((tm, d), lambda i: (i, 0)),
                 pl.BlockSpec((1, d), lambda i: (0, 0)), pl.BlockSpec((1, d), lambda i: (0, 0))]
    return pl.pallas_call(
        functools.partial(_outproj_ln_kernel, n_parts=len(parts), alpha=alpha),
        grid=(t // tm,),
        in_specs=in_specs,
        out_specs=pl.BlockSpec((tm, d), lambda i: (i, 0)),
        out_shape=jax.ShapeDtypeStruct((t, d), F32),
        compiler_params=_cparams(1),
        name="outproj_ln",
    )(*parts, *ws, x, g.reshape(1, d).astype(F32), b.reshape(1, d).astype(F32))


def _cross_router_kernel(x_ref, wq_ref, k_ref, v_ref, wo_ref, g_ref, b_ref, wr_ref, br_ref, o_ref, r_ref,
                         *, alpha, scale, n_heads, n_groups, per_group):
    x = x_ref[...]
    q = (jnp.dot(x.astype(BF16), wq_ref[...], preferred_element_type=F32) * scale).astype(BF16)
    dh = q.shape[1] // n_heads
    outs = []
    for hh in range(n_heads):
        cols = slice(hh * dh, (hh + 1) * dh)
        s = lax.dot_general(q[:, cols], k_ref[:, cols], (((1,), (1,)), ((), ())), preferred_element_type=F32)
        e = jnp.exp(s - jnp.max(s, axis=-1, keepdims=True))
        pr = e / jnp.sum(e, axis=-1, keepdims=True)
        outs.append(jnp.dot(pr.astype(BF16), v_ref[:, cols], preferred_element_type=F32))
    o = jnp.concatenate(outs, axis=-1).astype(BF16)
    y = _layer_norm(alpha * x + jnp.dot(o, wo_ref[...], preferred_element_type=F32), g_ref[...], b_ref[...])
    o_ref[...] = y

    logits = jnp.dot(y.astype(BF16), wr_ref[...], preferred_element_type=F32) + br_ref[...]
    lane = lax.broadcasted_iota(jnp.int32, logits.shape, 1)
    far = jnp.int32(LANES)
    is_g = lane < n_groups
    gl = jnp.where(is_g, logits, -jnp.inf)
    gmax = jnp.max(gl, axis=-1, keepdims=True)
    grp = jnp.min(jnp.where(is_g & (gl == gmax), lane, far), axis=-1, keepdims=True)
    g_gate = 1.0 / jnp.sum(jnp.where(is_g, jnp.exp(gl - gmax), 0.0), axis=-1, keepdims=True)
    lo = n_groups + per_group * grp
    is_e = (lane >= lo) & (lane < lo + per_group)
    el = jnp.where(is_e, logits, -jnp.inf)
    v1 = jnp.max(el, axis=-1, keepdims=True)
    i1 = jnp.min(jnp.where(is_e & (el == v1), lane, far), axis=-1, keepdims=True)
    is_e2 = is_e & (lane != i1)
    el2 = jnp.where(is_e2, logits, -jnp.inf)
    v2 = jnp.max(el2, axis=-1, keepdims=True)
    i2 = jnp.min(jnp.where(is_e2 & (el2 == v2), lane, far), axis=-1, keepdims=True)
    e2 = jnp.exp(v2 - v1)
    w1 = g_gate / (1.0 + e2)
    w2 = g_gate * e2 / (1.0 + e2)
    r = jnp.where(lane == 0, (i1 - n_groups).astype(F32),
                  jnp.where(lane == 1, (i2 - n_groups).astype(F32),
                            jnp.where(lane == 2, w1, jnp.where(lane == 3, w2, 0.0))))
    r_ref[...] = r


def _cross_router(x, kv_bf16, wq_bf16, wo_bf16, g, b, wr_bf16, br, *, seq, mem_len, alpha, per_group):
    t, d = x.shape
    cw = wq_bf16.shape[1]
    tm = min(TM_PROJ, seq)
    assert seq % tm == 0
    tps = seq // tm
    dh = cw // CROSS_HEADS
    n_exp = wr_bf16.shape[1]
    return pl.pallas_call(
        functools.partial(_cross_router_kernel, alpha=alpha, scale=dh ** -0.5, n_heads=CROSS_HEADS,
                          n_groups=N_GROUPS, per_group=per_group),
        grid=(t // tm,),
        in_specs=[
            pl.BlockSpec((tm, d), lambda i: (i, 0)),
            pl.BlockSpec((d, cw), lambda i: (0, 0)),
            pl.BlockSpec((mem_len, cw), lambda i: (i // tps, 0)),
            pl.BlockSpec((mem_len, cw), lambda i: (i // tps, 1)),
            pl.BlockSpec((cw, d), lambda i: (0, 0)),
            pl.BlockSpec((1, d), lambda i: (0, 0)),
            pl.BlockSpec((1, d), lambda i: (0, 0)),
            pl.BlockSpec((d, n_exp), lambda i: (0, 0)),
            pl.BlockSpec((1, n_exp), lambda i: (0, 0)),
        ],
        out_specs=[pl.BlockSpec((tm, d), lambda i: (i, 0)), pl.BlockSpec((tm, LANES), lambda i: (i, 0))],
        out_shape=[jax.ShapeDtypeStruct((t, d), F32), jax.ShapeDtypeStruct((t, LANES), F32)],
        compiler_params=_cparams(1),
        name="cross_attn_router",
    )(x, wq_bf16, kv_bf16, kv_bf16, wo_bf16, g.reshape(1, d).astype(F32), b.reshape(1, d).astype(F32),
      wr_bf16, br)


def _moe_kernel(be_ref, nu_ref, tok0_ref, tokn_ref, dstp_ref, dstc_ref, gw_ref, w1_ref, w3_ref, w2_ref,
                x_hbm, o_hbm, xb0, xb1, yb0, yb1, wb1, wb3, wb2, gsem, ssem):
    i = pl.program_id(0)
    n_used = nu_ref[0]
    tb = xb0.shape[0]
    xbufs, ybufs = (xb0, xb1), (yb0, yb1)

    def gather_row(tok, r, s):
        return pltpu.make_async_copy(x_hbm.at[pl.ds(tok, 1), :], xbufs[s].at[pl.ds(r, 1), :], gsem.at[s])

    def scatter_row(dst, r, s):
        return pltpu.make_async_copy(ybufs[s].at[pl.ds(r, 1), :], o_hbm.at[pl.ds(dst, 1), :], ssem.at[s])

    def wait_gather(s):
        pltpu.make_async_copy(x_hbm.at[pl.ds(0, tb), :], xbufs[s], gsem.at[s]).wait()

    def wait_scatter(s):
        pltpu.make_async_copy(ybufs[s], o_hbm.at[pl.ds(0, tb), :], ssem.at[s]).wait()

    def step(slot):
        other = 1 - slot

        @pl.when(i == 0)
        def _():
            def body(r, c):
                gather_row(tok0_ref[0, 0, r], r, 0).start()
                return c
            lax.fori_loop(0, tb, body, 0, unroll=8)
            yb1[...] = jnp.zeros(yb1.shape, F32)

        first_of_expert = jnp.logical_or(i == 0, be_ref[i] != be_ref[jnp.maximum(i - 1, 0)])

        @pl.when(first_of_expert)
        def _():
            wb1[...] = w1_ref[0, 0].astype(BF16)
            wb3[...] = w3_ref[0, 0].astype(BF16)
            wb2[...] = w2_ref[0, 0].astype(BF16)

        wait_gather(slot)
        for r in range(tb):
            gather_row(tokn_ref[0, 0, r], r, other).start(priority=r % 2)
        for r in range(tb):
            scatter_row(dstp_ref[0, 0, r], r, other).start(priority=r % 2)
        xb = xbufs[slot][...].astype(BF16)
        h1 = jnp.dot(xb, wb1[...], preferred_element_type=F32)
        h3 = jnp.dot(xb, wb3[...], preferred_element_type=F32)
        hb = (h1 * jax.nn.sigmoid(h1) * h3).astype(BF16)
        ybufs[slot][...] = jnp.dot(hb, wb2[...], preferred_element_type=F32) * gw_ref[...]
        wait_scatter(other)

        @pl.when(i == n_used - 1)
        def _():
            def body(r, c):
                scatter_row(dstc_ref[0, 0, r], r, slot).start()
                return c
            lax.fori_loop(0, tb, body, 0, unroll=8)
            wait_scatter(slot)
            wait_gather(other)

    @pl.when(jnp.logical_and(i < n_used, (i & 1) == 0))
    def _():
        step(0)

    @pl.when(jnp.logical_and(i < n_used, (i & 1) == 1))
    def _():
        step(1)


def _moe_experts(x, blk_e, n_used, tok, dst, gw, w1, w3, w2, *, layer, n_out_rows):
    t, d = x.shape
    nb = blk_e.shape[0]
    tb = tok.shape[0] // nb
    ff = w1.shape[-1]
    tok3 = tok.reshape(nb, 1, tb)
    dst3 = dst.reshape(nb, 1, tb)
    smem_blk = lambda f: pl.BlockSpec((1, 1, tb), f, memory_space=pltpu.SMEM)
    grid_spec = pltpu.PrefetchScalarGridSpec(
        num_scalar_prefetch=2,
        grid=(nb,),
        in_specs=[
            smem_blk(lambda i, be, nu: (0, 0, 0)),
            smem_blk(lambda i, be, nu: (jnp.minimum(i + 1, nb - 1), 0, 0)),
            smem_blk(lambda i, be, nu: (jnp.where(i == 0, nb - 1, i - 1), 0, 0)),
            smem_blk(lambda i, be, nu: (i, 0, 0)),
            pl.BlockSpec((tb, 1), lambda i, be, nu: (i, 0)),
            pl.BlockSpec((1, 1, d, ff), lambda i, be, nu: (layer, be[i], 0, 0)),
            pl.BlockSpec((1, 1, d, ff), lambda i, be, nu: (layer, be[i], 0, 0)),
            pl.BlockSpec((1, 1, ff, d), lambda i, be, nu: (layer, be[i], 0, 0)),
            pl.BlockSpec(memory_space=pl.ANY),
        ],
        out_specs=pl.BlockSpec(memory_space=pl.ANY),
        scratch_shapes=[pltpu.VMEM((tb, d), F32), pltpu.VMEM((tb, d), F32),
                        pltpu.VMEM((tb, d), F32), pltpu.VMEM((tb, d), F32),
                        pltpu.VMEM((d, ff), BF16), pltpu.VMEM((d, ff), BF16), pltpu.VMEM((ff, d), BF16),
                        pltpu.SemaphoreType.DMA((2,)), pltpu.SemaphoreType.DMA((2,))],
    )
    return pl.pallas_call(
        _moe_kernel,
        grid_spec=grid_spec,
        out_shape=jax.ShapeDtypeStruct((n_out_rows, d), F32),
        compiler_params=_cparams(1),
        name="moe_experts",
    )(blk_e, n_used, tok3, tok3, dst3, dst3, gw, w1, w3, w2, x)


def _moe_dispatch(route, *, n_experts, tb):
    t = route.shape[0]
    n_assign = TOP_K * t
    nb = n_assign // tb + n_experts
    eid = route[:, :TOP_K].astype(jnp.int32).reshape(-1)
    wgt = route[:, TOP_K:2 * TOP_K].reshape(-1)
    assert n_experts * n_assign < 2 ** 31
    order = jnp.sort(eid * n_assign + jnp.arange(n_assign, dtype=jnp.int32)) % n_assign
    counts = jnp.sum((eid[:, None] == jnp.arange(n_experts, dtype=jnp.int32)[None, :]).astype(jnp.int32), axis=0)
    blocks = (counts + tb - 1) // tb
    bend = jnp.cumsum(blocks)
    start = jnp.cumsum(counts) - counts
    blk = jnp.arange(nb, dtype=jnp.int32)
    blk_e = jnp.minimum(jnp.sum((blk[:, None] >= bend[None, :]).astype(jnp.int32), axis=1), n_experts - 1)
    k_in_e = blk - (bend - blocks)[blk_e]
    n_valid = jnp.clip(counts[blk_e] - k_in_e * tb, 0, tb)
    r = jnp.arange(tb, dtype=jnp.int32)
    valid = r[None, :] < n_valid[:, None]
    src = jnp.where(valid, (start[blk_e] + k_in_e * tb)[:, None] + r[None, :], 0)
    a = order[src]
    a_tok = a // TOP_K
    tok = jnp.where(valid, a_tok, 0).reshape(-1)
    dst = jnp.where(valid, (a % TOP_K) * t + a_tok, TOP_K * t + r[None, :]).reshape(-1)
    gw = jnp.where(valid, wgt[a], 0.0).reshape(-1, 1)
    return blk_e.astype(jnp.int32), bend[-1:].astype(jnp.int32), tok, dst, gw


def _combine_ln_kernel(x_ref, y0_ref, y1_ref, g_ref, b_ref, o_ref, *, alpha):
    o_ref[...] = _layer_norm(alpha * x_ref[...] + y0_ref[...] + y1_ref[...], g_ref[...], b_ref[...])


def _combine_ln(x, y, g, b, *, alpha):
    t, d = x.shape
    tm = min(TM_PROJ, t)
    assert t % tm == 0
    nt = t // tm
    return pl.pallas_call(
        functools.partial(_combine_ln_kernel, alpha=alpha),
        grid=(nt,),
        in_specs=[pl.BlockSpec((tm, d), lambda i: (i, 0)),
                  pl.BlockSpec((tm, d), lambda i: (i, 0)),
                  pl.BlockSpec((tm, d), lambda i: (i + nt, 0)),
                  pl.BlockSpec((1, d), lambda i: (0, 0)), pl.BlockSpec((1, d), lambda i: (0, 0))],
        out_specs=pl.BlockSpec((tm, d), lambda i: (i, 0)),
        out_shape=jax.ShapeDtypeStruct((t, d), F32),
        compiler_params=_cparams(1),
        name="moe_combine_ln",
    )(x, y, y, g.reshape(1, d).astype(F32), b.reshape(1, d).astype(F32))


def _rope_tables(positions, dh):
    rope_dim = dh // 4
    half = rope_dim // 2
    assert half == ROPE_HALF and dh == LANES
    inv_freq = ROPE_THETA ** (-jnp.arange(0, rope_dim, 2, dtype=F32) / rope_dim)
    ang = positions.astype(F32).reshape(-1, 1) * inv_freq[None, :]
    cos, sin = jnp.cos(ang), jnp.sin(ang)
    t = ang.shape[0]
    c = jnp.concatenate([cos, cos, jnp.ones((t, dh - rope_dim), F32)], axis=1)
    s1 = jnp.concatenate([-sin, jnp.zeros((t, dh - half), F32)], axis=1)
    s2 = jnp.concatenate([jnp.zeros((t, half), F32), sin, jnp.zeros((t, dh - rope_dim), F32)], axis=1)
    return c, s1, s2


def kernel(x, mem, positions, w_in, w_out, ln_mix_g, ln_mix_b, ln_mem_g, ln_mem_b, ln_ffn_g, ln_ffn_b, lam_q1, lam_k1, lam_q2, lam_k2, diff_subln_g, conv_w, conv_b, conv_ln_g, conv_ln_b, sgu_ln_g, sgu_ln_b, sgu_w, sgu_b, sc_w, mem_kv_w, xq_w, xo_w, rg_w, rg_b, re_w, re_b, e_w1, e_w3, e_w2):
    bsz, seq, d = x.shape
    t = bsz * seq
    depth = w_in.shape[0]
    mem_len = mem.shape[1]
    half = w_in.shape[2] // 5
    dh = lam_q1.shape[-1]
    n_experts = re_w.shape[-1]
    alpha = (2 * depth) ** 0.25
    assert half % TN_PROJ == 0 and diff_subln_g.shape[-1] == 2 * dh and half == 2 * DIFF_HEADS * dh

    xt = x.reshape(t, d)
    rope = _rope_tables(positions, dh)
    kv = _inproj(mem.reshape(bsz * mem_len, d), mem_kv_w.astype(BF16))

    for l in range(depth):
        j = l // 2
        w_l = w_in[l].astype(BF16)
        if l % 2 == 0:
            h = _inproj(xt, w_l, rope, n_rope_tiles=2 * half // TN_PROJ, n_scale_tiles=half // TN_PROJ,
                        scale=dh ** -0.5 * math.log2(math.e))
            lam_init = 0.8 - 0.6 * math.exp(-0.3 * l)
            lam = (jnp.exp(jnp.sum(lam_q1[j].astype(F32) * lam_k1[j].astype(F32)))
                   - jnp.exp(jnp.sum(lam_q2[j].astype(F32) * lam_k2[j].astype(F32))) + lam_init)
            o = _diff_attention(h, lam, diff_subln_g[j], bsz=bsz, seq=seq, out_scale=1.0 - lam_init)
            c = _conformer(h, conv_w[j], conv_b[j], conv_ln_g[j], conv_ln_b[j], seq=seq, col_a=3, col_g=4)
            parts = [o, c]
        else:
            h = _inproj(xt, w_l)
            parts = [_sgu_shortconv(h, sgu_ln_g[j], sgu_ln_b[j], sgu_w[j], sgu_b[j], sc_w[j], seq=seq)]
        xt = _outproj_ln(parts, w_out[l].astype(BF16), xt, ln_mix_g[l], ln_mix_b[l], alpha=alpha)

        n_route = N_GROUPS + n_experts
        wr = jnp.concatenate([rg_w[l], re_w[l], jnp.zeros((d, LANES - n_route), F32)], axis=1).astype(BF16)
        br = jnp.concatenate([rg_b[l], re_b[l], jnp.zeros((LANES - n_route,), F32)]).reshape(1, LANES).astype(F32)
        xt, route = _cross_router(xt, kv, xq_w[l].astype(BF16), xo_w[l].astype(BF16), ln_mem_g[l], ln_mem_b[l],
                                  wr, br, seq=seq, mem_len=mem_len, alpha=alpha, per_group=n_experts // N_GROUPS)

        blk_e, n_used, tok, dst, gw = _moe_dispatch(route, n_experts=n_experts, tb=TB_MOE)
        y = _moe_experts(xt, blk_e, n_used, tok, dst, gw, e_w1, e_w3, e_w2, layer=l, n_out_rows=TOP_K * t + TB_MOE)
        xt = _combine_ln(xt, y, ln_ffn_g[l], ln_ffn_b[l], alpha=alpha)
    return xt.reshape(bsz, seq, d)
```

```python
import functools
import math

import jax
import jax.numpy as jnp
from jax import lax
from jax.experimental import pallas as pl
from jax.experimental.pallas import tpu as pltpu

F32 = jnp.float32
BF16 = jnp.bfloat16

DIFF_HEADS = 4
ROPE_THETA = 500000.0
CONF_KERNEL = 31
SGU_CHUNK = 128
CROSS_HEADS = 4
N_GROUPS = 4
TOP_K = 2
LN_EPS = 1e-5

LANES = 128
SUBLANES = 8
NEG_BIG = -1e30
VMEM_LIMIT = 56 * 1024 * 1024

TM_PROJ = 512
TM_INPROJ = 1024
TN_PROJ = 1024
TQ_ATT = 512
ATT_RC = 64
TR_MIX = 256
CONV_HALO = 32
CONV_RB = 32
SC_HALO = 16
TB_MOE = 256


def _cparams(n_axes):
    return pltpu.CompilerParams(dimension_semantics=("arbitrary",) * n_axes,
                                vmem_limit_bytes=VMEM_LIMIT)


def _layer_norm(y, g, b):
    mu = jnp.mean(y, axis=-1, keepdims=True)
    d = y - mu
    var = jnp.mean(d * d, axis=-1, keepdims=True)
    return d * lax.rsqrt(var + LN_EPS) * g + b


def _inproj_kernel(x_ref, w_ref, *rest, n_rope_tiles, n_scale_tiles, scale):
    if n_rope_tiles:
        c_ref, s1_ref, s2_ref, o_ref, xb_ref = rest
    else:
        o_ref, xb_ref = rest
    j = pl.program_id(1)

    @pl.when(j == 0)
    def _():
        xb_ref[...] = x_ref[...].astype(BF16)

    acc = jnp.dot(xb_ref[...], w_ref[...], preferred_element_type=F32)
    if not n_rope_tiles:
        o_ref[...] = acc.astype(o_ref.dtype)
        return

    @pl.when(j >= n_rope_tiles)
    def _():
        o_ref[...] = acc.astype(o_ref.dtype)

    @pl.when(j < n_rope_tiles)
    def _():
        a = acc * jnp.where(j < n_scale_tiles, scale, 1.0).astype(F32)
        c, s1, s2 = c_ref[...], s1_ref[...], s2_ref[...]
        for g in range(a.shape[1] // LANES):
            ag = a[:, g * LANES:(g + 1) * LANES]
            og = (ag * c + pltpu.roll(ag, LANES - ROPE_HALF, axis=1) * s1
                  + pltpu.roll(ag, ROPE_HALF, axis=1) * s2)
            o_ref[:, g * LANES:(g + 1) * LANES] = og.astype(o_ref.dtype)


ROPE_HALF = 16


def _inproj(x, w_bf16, rope=None, *, n_rope_tiles=0, n_scale_tiles=0, scale=1.0, out_dtype=BF16):
    m, k = x.shape
    n = w_bf16.shape[1]
    tm = min(TM_INPROJ, m)
    tn = min(TN_PROJ, n)
    assert m % tm == 0 and n % tn == 0
    in_specs = [pl.BlockSpec((tm, k), lambda i, j: (i, 0)),
                pl.BlockSpec((k, tn), lambda i, j: (0, j))]
    args = [x, w_bf16]
    if n_rope_tiles:
        in_specs += [pl.BlockSpec((tm, LANES), lambda i, j: (i, 0))] * 3
        args += list(rope)
    return pl.pallas_call(
        functools.partial(_inproj_kernel, n_rope_tiles=n_rope_tiles, n_scale_tiles=n_scale_tiles, scale=scale),
        grid=(m // tm, n // tn),
        in_specs=in_specs,
        out_specs=pl.BlockSpec((tm, tn), lambda i, j: (i, j)),
        out_shape=jax.ShapeDtypeStruct((m, n), out_dtype),
        scratch_shapes=[pltpu.VMEM((tm, k), BF16)],
        compiler_params=_cparams(2),
        name="inproj_rope" if n_rope_tiles else "inproj",
    )(*args)


def _diff_attn_kernel(lam_ref, q_ref, k_ref, v_ref, g_ref, o_ref, s_a, s_b, x_a, x_b, p_a, p_b, a_a, a_b,
                      m_ref, l_ref, acc_ref, *, dh, out_scale):
    qi = pl.program_id(2)
    tq = q_ref.shape[0]
    tk = tq
    dv = v_ref.shape[1]
    ngrp = tk // LANES
    m_ref[...] = jnp.full(m_ref.shape, NEG_BIG, F32)
    l_ref[...] = jnp.zeros(l_ref.shape, F32)
    acc_ref[...] = jnp.zeros(acc_ref.shape, F32)

    def scores(j, s_ref, x_ref):
        r0 = pl.multiple_of(j * tk, tk)
        for mi in range(2):
            cols = slice(mi * dh, (mi + 1) * dh)
            s = lax.dot_general(q_ref[:, cols], k_ref[pl.ds(r0, tk), cols], (((1,), (1,)), ((), ())),
                                preferred_element_type=F32)
            s_ref[mi] = s
            mx = s[:, :LANES]
            for g in range(1, ngrp):
                mx = jnp.maximum(mx, s[:, g * LANES:(g + 1) * LANES])
            x_ref[mi] = mx

    def softmax(s_ref, x_ref, p_ref, a_ref, masked):
        for mi in range(2):
            for rc in range(tq // ATT_RC):
                rows = slice(rc * ATT_RC, (rc + 1) * ATT_RC)
                groups = [s_ref[mi, rows, g * LANES:(g + 1) * LANES] for g in range(ngrp)]
                if masked:
                    row = lax.broadcasted_iota(jnp.int32, (ATT_RC, LANES), 0) + rc * ATT_RC
                    col = lax.broadcasted_iota(jnp.int32, (ATT_RC, LANES), 1)
                    groups = [jnp.where(col + g * LANES <= row, sg, NEG_BIG) for g, sg in enumerate(groups)]
                    mx = functools.reduce(jnp.maximum, groups)
                else:
                    mx = x_ref[mi, rows]
                m_prev = m_ref[mi, rows]
                m_new = jnp.maximum(m_prev, jnp.max(mx, axis=1, keepdims=True))
                alpha = jnp.exp2(m_prev - m_new)
                lsum = None
                for g, sg in enumerate(groups):
                    pg = jnp.exp2(sg - m_new)
                    lsum = pg if lsum is None else lsum + pg
                    p_ref[mi, rows, g * LANES:(g + 1) * LANES] = pg.astype(BF16)
                l_ref[mi, rows] = alpha * l_ref[mi, rows] + lsum
                m_ref[mi, rows] = m_new
                a_ref[mi, rows] = alpha

    def weighted_values(j, p_ref, a_ref):
        r0 = pl.multiple_of(j * tk, tk)
        v = v_ref[pl.ds(r0, tk), :]
        for mi in range(2):
            pv = jnp.dot(p_ref[mi], v, preferred_element_type=F32)
            alpha = a_ref[mi]
            acc_ref[mi] = jnp.concatenate(
                [acc_ref[mi, :, c * LANES:(c + 1) * LANES] * alpha for c in range(dv // LANES)], axis=1) + pv

    scores(0, s_a, x_a)

    def pair(u, carry):
        j = 2 * u
        scores(j + 1, s_b, x_b)
        softmax(s_a, x_a, p_a, a_a, False)
        weighted_values(j, p_a, a_a)
        scores(j + 2, s_a, x_a)
        softmax(s_b, x_b, p_b, a_b, False)
        weighted_values(j + 1, p_b, a_b)
        return carry

    lax.fori_loop(0, lax.shift_right_logical(qi, 1), pair, 0)

    @pl.when((qi & 1) == 0)
    def _():
        softmax(s_a, x_a, p_a, a_a, True)
        weighted_values(qi, p_a, a_a)

    @pl.when((qi & 1) == 1)
    def _():
        scores(qi, s_b, x_b)
        softmax(s_a, x_a, p_a, a_a, False)
        weighted_values(qi - 1, p_a, a_a)
        softmax(s_b, x_b, p_b, a_b, True)
        weighted_values(qi, p_b, a_b)

    lam = lam_ref[0, 0]
    l0 = jnp.sum(l_ref[0], axis=1, keepdims=True)
    l1 = jnp.sum(l_ref[1], axis=1, keepdims=True)
    o = acc_ref[0] / l0 - lam * (acc_ref[1] / l1)
    ms = jnp.mean(o * o, axis=-1, keepdims=True)
    o_ref[...] = (o * lax.rsqrt(ms + LN_EPS) * g_ref[...] * out_scale).astype(o_ref.dtype)


def _diff_attention(h, lam, subln_g, *, bsz, seq, out_scale):
    t = h.shape[0]
    dv = subln_g.shape[-1]
    dh = dv // 2
    tq = min(TQ_ATT, seq)
    nq = seq // tq
    assert seq % tq == 0
    hd_n = DIFF_HEADS
    return pl.pallas_call(
        functools.partial(_diff_attn_kernel, dh=dh, out_scale=out_scale),
        grid=(bsz, hd_n, nq),
        in_specs=[
            pl.BlockSpec(memory_space=pltpu.SMEM),
            pl.BlockSpec((tq, dv), lambda b, hd, qi: (b * nq + qi, hd)),
            pl.BlockSpec((seq, dv), lambda b, hd, qi: (b, hd_n + hd)),
            pl.BlockSpec((seq, dv), lambda b, hd, qi: (b, 2 * hd_n + hd)),
            pl.BlockSpec((1, dv), lambda b, hd, qi: (0, 0)),
        ],
        out_specs=pl.BlockSpec((tq, dv), lambda b, hd, qi: (b * nq + qi, hd)),
        out_shape=jax.ShapeDtypeStruct((t, hd_n * dv), BF16),
        scratch_shapes=[pltpu.VMEM((2, tq, tq), F32), pltpu.VMEM((2, tq, tq), F32),
                        pltpu.VMEM((2, tq, LANES), F32), pltpu.VMEM((2, tq, LANES), F32),
                        pltpu.VMEM((2, tq, tq), BF16), pltpu.VMEM((2, tq, tq), BF16),
                        pltpu.VMEM((2, tq, LANES), F32), pltpu.VMEM((2, tq, LANES), F32),
                        pltpu.VMEM((2, tq, LANES), F32), pltpu.VMEM((2, tq, LANES), F32),
                        pltpu.VMEM((2, tq, dv), F32)],
        compiler_params=_cparams(3),
        name="diff_attn",
    )(lam.reshape(1, 1).astype(F32), h, h, h, subln_g.reshape(1, dv).astype(F32))


def _conformer_kernel(a_ref, g_ref, ah_ref, gh_ref, w_ref, cb_ref, lg_ref, lb_ref, o_ref, cext_ref, conv_ref,
                      *, tiles_per_seq):
    i = pl.program_id(0)
    tr, width = a_ref.shape
    halo = ah_ref.shape[0]
    ksz = w_ref.shape[0]
    first = (i % tiles_per_seq) == 0

    glu_h = ah_ref[...].astype(F32) * jax.nn.sigmoid(gh_ref[...].astype(F32))
    cext_ref[0:halo, :] = jnp.where(first, 0.0, glu_h)
    cext_ref[halo:halo + tr, :] = a_ref[...].astype(F32) * jax.nn.sigmoid(g_ref[...].astype(F32))
    cext_ref[halo + tr:halo + tr + SUBLANES, :] = jnp.zeros((SUBLANES, width), F32)

    base = halo - (ksz - 1)
    win = CONV_RB + halo + SUBLANES

    def chunk(rc, carry):
        r0 = pl.multiple_of(rc * CONV_RB, CONV_RB)
        for c in range(width // LANES):
            lanes = slice(c * LANES, (c + 1) * LANES)
            wnd = cext_ref[pl.ds(r0, win), lanes]
            acc = jnp.zeros((CONV_RB, LANES), F32)
            for b in range(SUBLANES):
                shifted = wnd if b == 0 else pltpu.roll(wnd, win - b, axis=0)
                for a in range((halo + SUBLANES) // SUBLANES):
                    j = SUBLANES * a + b - base
                    if 0 <= j < ksz:
                        acc = acc + w_ref[j:j + 1, lanes] * shifted[SUBLANES * a:SUBLANES * a + CONV_RB]
            conv_ref[pl.ds(r0, CONV_RB), lanes] = acc + cb_ref[:, lanes]
        return carry

    lax.fori_loop(0, tr // CONV_RB, chunk, 0)
    y = _layer_norm(conv_ref[...], lg_ref[...], lb_ref[...])
    o_ref[...] = (y * jax.nn.sigmoid(y)).astype(o_ref.dtype)


def _conformer(h, conv_w, conv_b, ln_g, ln_b, *, seq, col_a, col_g):
    t = h.shape[0]
    ksz, width = conv_w.shape
    tr = min(TR_MIX, seq)
    halo = CONV_HALO
    assert seq % tr == 0 and tr % halo == 0 and ksz - 1 <= halo and tr % CONV_RB == 0
    rpb = tr // halo
    row = lambda v: v.reshape(1, width).astype(F32)
    return pl.pallas_call(
        functools.partial(_conformer_kernel, tiles_per_seq=seq // tr),
        grid=(t // tr,),
        in_specs=[
            pl.BlockSpec((tr, width), lambda i: (i, col_a)),
            pl.BlockSpec((tr, width), lambda i: (i, col_g)),
            pl.BlockSpec((halo, width), lambda i: (jnp.maximum(i * rpb - 1, 0), col_a)),
            pl.BlockSpec((halo, width), lambda i: (jnp.maximum(i * rpb - 1, 0), col_g)),
            pl.BlockSpec((ksz, width), lambda i: (0, 0)),
            pl.BlockSpec((1, width), lambda i: (0, 0)),
            pl.BlockSpec((1, width), lambda i: (0, 0)),
            pl.BlockSpec((1, width), lambda i: (0, 0)),
        ],
        out_specs=pl.BlockSpec((tr, width), lambda i: (i, 0)),
        out_shape=jax.ShapeDtypeStruct((t, width), BF16),
        scratch_shapes=[pltpu.VMEM((halo + tr + SUBLANES, width), F32), pltpu.VMEM((tr, width), F32)],
        compiler_params=_cparams(1),
        name="conformer_conv",
    )(h, h, h, h, conv_w.astype(F32), row(conv_b), row(ln_g), row(ln_b))


def _gelu_exact(x):
    return 0.5 * x * (1.0 + lax.erf(x * math.sqrt(0.5)))


def _sgu_kernel(u_ref, v_ref, gb_ref, gc_ref, xi_ref, gch_ref, xih_ref, lg_ref, lb_ref, sw_ref, sb_ref, cw_ref,
                o_ref, pext_ref, *, tiles_per_seq):
    i = pl.program_id(0)
    tr, width = u_ref.shape
    halo = gch_ref.shape[0]
    n_grp, chunk, _ = sw_ref.shape
    gdim = width // n_grp
    first = (i % tiles_per_seq) == 0

    vg = _layer_norm(_gelu_exact(v_ref[...].astype(F32)), lg_ref[...], lb_ref[...]).astype(BF16)
    trow = lax.broadcasted_iota(jnp.int32, (chunk, chunk), 0)
    tcol = lax.broadcasted_iota(jnp.int32, (chunk, chunk), 1)
    for g in range(n_grp):
        wg = jnp.where(tcol <= trow, sw_ref[g], 0.0).astype(BF16)
        for n in range(tr // chunk):
            rows = slice(n * chunk, (n + 1) * chunk)
            cols = slice(g * gdim, (g + 1) * gdim)
            sv = jnp.dot(wg, vg[rows, cols], preferred_element_type=F32) + sb_ref[g]
            o_ref[rows, cols] = (_gelu_exact(u_ref[rows, cols].astype(F32)) * sv).astype(o_ref.dtype)

    ph = gch_ref[...].astype(F32) * xih_ref[...].astype(F32)
    pext_ref[0:halo, :] = jnp.where(first, 0.0, ph)
    pext_ref[halo:halo + tr, :] = gc_ref[...].astype(F32) * xi_ref[...].astype(F32)
    pe = pext_ref[...]
    n_ext = halo + tr
    ksz = cw_ref.shape[0]
    conv = cw_ref[ksz - 1:ksz, :] * pe[halo:]
    for back in range(1, ksz):
        conv = conv + cw_ref[ksz - 1 - back:ksz - back, :] * pltpu.roll(pe, back, axis=0)[halo:]
    del n_ext
    o_ref[:, width:2 * width] = (gb_ref[...].astype(F32) * conv).astype(o_ref.dtype)


def _sgu_shortconv(h, ln_g, ln_b, sgu_w, sgu_b, sc_w, *, seq):
    t = h.shape[0]
    width = ln_g.shape[-1]
    n_grp, chunk, _ = sgu_w.shape
    gdim = width // n_grp
    tr = min(TR_MIX, seq)
    halo = SC_HALO
    assert seq % tr == 0 and tr % chunk == 0 and tr % halo == 0 and sc_w.shape[0] - 1 <= halo
    rpb = tr // halo
    row = lambda v: v.reshape(1, width).astype(F32)
    bias = jnp.broadcast_to(sgu_b.astype(F32)[:, :, None], (n_grp, chunk, gdim))
    cur = lambda c: pl.BlockSpec((tr, width), lambda i: (i, c))
    prev = lambda c: pl.BlockSpec((halo, width), lambda i: (jnp.maximum(i * rpb - 1, 0), c))
    whole = lambda a: pl.BlockSpec(a.shape, lambda i: (0,) * a.ndim)
    small = [row(ln_g), row(ln_b), sgu_w.astype(F32), bias, sc_w.astype(F32)]
    return pl.pallas_call(
        functools.partial(_sgu_kernel, tiles_per_seq=seq // tr),
        grid=(t // tr,),
        in_specs=[cur(0), cur(1), cur(2), cur(3), cur(4), prev(3), prev(4)] + [whole(a) for a in small],
        out_specs=pl.BlockSpec((tr, 2 * width), lambda i: (i, 0)),
        out_shape=jax.ShapeDtypeStruct((t, 2 * width), BF16),
        scratch_shapes=[pltpu.VMEM((halo + tr, width), F32)],
        compiler_params=_cparams(1),
        name="sgu_shortconv",
    )(h, h, h, h, h, h, h, *small)


def _mix_cross_router_kernel(*refs, n_parts, alpha, scale, n_heads, n_groups, per_group):
    parts = refs[:n_parts]
    ws = refs[n_parts:2 * n_parts]
    x_ref, g1_ref, b1_ref = refs[2 * n_parts:2 * n_parts + 3]
    acc = alpha * x_ref[...]
    for p_ref, w_ref in zip(parts, ws):
        acc = acc + jnp.dot(p_ref[...], w_ref[...], preferred_element_type=F32)
    x1 = _layer_norm(acc, g1_ref[...], b1_ref[...])
    _cross_route(x1, *refs[2 * n_parts + 3:], alpha=alpha, scale=scale, n_heads=n_heads, n_groups=n_groups,
                 per_group=per_group)


def _cross_route(x, wq_ref, k_ref, v_ref, wo_ref, g_ref, b_ref, wr_ref, br_ref, o_ref, r_ref,
                 *, alpha, scale, n_heads, n_groups, per_group):
    q =(jnp.dot(x.astype(BF16), wq_ref[...], preferred_element_type=F32) * scale).astype(BF16)
    dh = q.shape[1] // n_heads
    outs = []
    for hh in range(n_heads):
        cols = slice(hh * dh, (hh + 1) * dh)
        s = lax.dot_general(q[:, cols], k_ref[:, cols], (((1,), (1,)), ((), ())), preferred_element_type=F32)
        e = jnp.exp(s - jnp.max(s, axis=-1, keepdims=True))
        pr = e / jnp.sum(e, axis=-1, keepdims=True)
        outs.append(jnp.dot(pr.astype(BF16), v_ref[:, cols], preferred_element_type=F32))
    o = jnp.concatenate(outs, axis=-1).astype(BF16)
    y = _layer_norm(alpha * x + jnp.dot(o, wo_ref[...], preferred_element_type=F32), g_ref[...], b_ref[...])
    o_ref[...] = y

    logits = jnp.dot(y.astype(BF16), wr_ref[...], preferred_element_type=F32) + br_ref[...]
    lane = lax.broadcasted_iota(jnp.int32, logits.shape, 1)
    far = jnp.int32(LANES)
    is_g = lane < n_groups
    gl = jnp.where(is_g, logits, -jnp.inf)
    gmax = jnp.max(gl, axis=-1, keepdims=True)
    grp = jnp.min(jnp.where(is_g & (gl == gmax), lane, far), axis=-1, keepdims=True)
    g_gate = 1.0 / jnp.sum(jnp.where(is_g, jnp.exp(gl - gmax), 0.0), axis=-1, keepdims=True)
    lo = n_groups + per_group * grp
    is_e = (lane >= lo) & (lane < lo + per_group)
    el = jnp.where(is_e, logits, -jnp.inf)
    v1 = jnp.max(el, axis=-1, keepdims=True)
    i1 = jnp.min(jnp.where(is_e & (el == v1), lane, far), axis=-1, keepdims=True)
    is_e2 = is_e & (lane != i1)
    el2 = jnp.where(is_e2, logits, -jnp.inf)
    v2 = jnp.max(el2, axis=-1, keepdims=True)
    i2 = jnp.min(jnp.where(is_e2 & (el2 == v2), lane, far), axis=-1, keepdims=True)
    e2 = jnp.exp(v2 - v1)
    w1 = g_gate / (1.0 + e2)
    w2 = g_gate * e2 / (1.0 + e2)
    r = jnp.where(lane == 0, (i1 - n_groups).astype(F32),
                  jnp.where(lane == 1, (i2 - n_groups).astype(F32),
                            jnp.where(lane == 2, w1, jnp.where(lane == 3, w2, 0.0))))
    r_ref[...] = r


def _mix_cross_router(parts, w_out_bf16, x, g1, b1, kv_bf16, wq_bf16, wo_bf16, g2, b2, wr_bf16, br,
                      *, seq, mem_len, alpha, per_group):
    t, d = x.shape
    cw = wq_bf16.shape[1]
    tm = min(TM_PROJ, seq)
    assert seq % tm == 0
    tps = seq // tm
    dh = cw // CROSS_HEADS
    n_exp = wr_bf16.shape[1]
    const = lambda shape: pl.BlockSpec(shape, lambda i: (0, 0), pipeline_mode=pl.Buffered(1))
    row = lambda v: v.reshape(1, d).astype(F32)
    in_specs, ws, off = [], [], 0
    for p in parts:
        kp = p.shape[1]
        in_specs.append(pl.BlockSpec((tm, kp), lambda i: (i, 0)))
        ws.append(w_out_bf16[off:off + kp])
        off += kp
    assert off == w_out_bf16.shape[0]
    in_specs += [const(w.shape) for w in ws]
    in_specs += [
        pl.BlockSpec((tm, d), lambda i: (i, 0)), const((1, d)), const((1, d)),
        const((d, cw)),
        pl.BlockSpec((mem_len, cw), lambda i: (i // tps, 0)),
        pl.BlockSpec((mem_len, cw), lambda i: (i // tps, 1)),
        const((cw, d)), const((1, d)), const((1, d)), const((d, n_exp)), const((1, n_exp)),
    ]
    return pl.pallas_call(
        functools.partial(_mix_cross_router_kernel, n_parts=len(parts), alpha=alpha, scale=dh ** -0.5,
                          n_heads=CROSS_HEADS, n_groups=N_GROUPS, per_group=per_group),
        grid=(t // tm,),
        in_specs=in_specs,
        out_specs=[pl.BlockSpec((tm, d), lambda i: (i, 0)), pl.BlockSpec((tm, LANES), lambda i: (i, 0))],
        out_shape=[jax.ShapeDtypeStruct((t, d), F32), jax.ShapeDtypeStruct((t, LANES), F32)],
        compiler_params=_cparams(1),
        name="mix_cross_router",
    )(*parts, *ws, x, row(g1), row(b1), wq_bf16, kv_bf16, kv_bf16, wo_bf16, row(g2), row(b2), wr_bf16, br)


def _moe_kernel(be_ref, nu_ref, idx0_ref, idx_ref, w1_ref, w3_ref, w2_ref,
                x_hbm, o_hbm, xb0, xb1, yb0, yb1, wb1, wb3, wb2, gsem, ssem):
    i = pl.program_id(0)
    n_used = nu_ref[0]
    tb = xb0.shape[0]
    xbufs, ybufs = (xb0, xb1), (yb0, yb1)

    def gather_row(tok, r, s):
        return pltpu.make_async_copy(x_hbm.at[pl.ds(tok, 1), :], xbufs[s].at[pl.ds(r, 1), :], gsem.at[s])

    def scatter_row(dst, r, s):
        return pltpu.make_async_copy(ybufs[s].at[pl.ds(r, 1), :], o_hbm.at[pl.ds(dst, 1), :], ssem.at[s])

    def wait_gather(s):
        pltpu.make_async_copy(x_hbm.at[pl.ds(0, tb), :], xbufs[s], gsem.at[s]).wait()

    def wait_scatter(s):
        pltpu.make_async_copy(ybufs[s], o_hbm.at[pl.ds(0, tb), :], ssem.at[s]).wait()

    def step(slot):
        other = 1 - slot

        @pl.when(i == 0)
        def _():
            def body(r, c):
                gather_row(idx0_ref[0, 0, r], r, 0).start()
                return c
            lax.fori_loop(0, tb, body, 0, unroll=8)
            yb1[...] = jnp.zeros(yb1.shape, F32)

        first_of_expert = jnp.logical_or(i == 0, be_ref[i] != be_ref[jnp.maximum(i - 1, 0)])

        @pl.when(first_of_expert)
        def _():
            wb1[...] = w1_ref[0, 0].astype(BF16)
            wb3[...] = w3_ref[0, 0].astype(BF16)
            wb2[...] = w2_ref[0, 0].astype(BF16)

        wait_gather(slot)
        for r in range(tb):
            gather_row(idx_ref[0, 0, r], r, other).start(priority=r % 2)
        for r in range(tb):
            scatter_row(idx_ref[0, 0, tb + r], r, other).start(priority=r % 2)
        xb = xbufs[slot][...].astype(BF16)
        h1 = jnp.dot(xb, wb1[...], preferred_element_type=F32)
        h3 = jnp.dot(xb, wb3[...], preferred_element_type=F32)
        hb = (h1 * jax.nn.sigmoid(h1) * h3).astype(BF16)
        ybufs[slot][...] = jnp.dot(hb, wb2[...], preferred_element_type=F32)
        wait_scatter(other)

    def drain(slot):
        other = 1 - slot
        wait_gather(slot)

        def body(r, c):
            scatter_row(idx_ref[0, 0, tb + r], r, other).start()
            return c
        lax.fori_loop(0, tb, body, 0, unroll=8)
        wait_scatter(other)

    for parity in range(2):
        @pl.when(jnp.logical_and(i < n_used, (i & 1) == parity))
        def _():
            step(parity)

        @pl.when(jnp.logical_and(i == n_used, (i & 1) == parity))
        def _():
            drain(parity)


def _moe_experts(x, blk_e, n_used, tok, dst, w1, w3, w2, *, layer, n_out_rows):
    t, d = x.shape
    nb = blk_e.shape[0]
    tb = tok.shape[0] // nb
    ff = w1.shape[-1]
    idx = jnp.concatenate([tok.reshape(nb, 1, tb), jnp.roll(dst.reshape(nb, 1, tb), 2, axis=0)], axis=-1)
    smem_blk = lambda f: pl.BlockSpec((1, 1, 2 * tb), f, memory_space=pltpu.SMEM)
    grid_spec = pltpu.PrefetchScalarGridSpec(
        num_scalar_prefetch=2,
        grid=(nb,),
        in_specs=[
            smem_blk(lambda i, be, nu: (0, 0, 0)),
            smem_blk(lambda i, be, nu: (jnp.minimum(i + 1, nb - 1), 0, 0)),
            pl.BlockSpec((1, 1, d, ff), lambda i, be, nu: (layer, be[i], 0, 0)),
            pl.BlockSpec((1, 1, d, ff), lambda i, be, nu: (layer, be[i], 0, 0)),
            pl.BlockSpec((1, 1, ff, d), lambda i, be, nu: (layer, be[i], 0, 0)),
            pl.BlockSpec(memory_space=pl.ANY),
        ],
        out_specs=pl.BlockSpec(memory_space=pl.ANY),
        scratch_shapes=[pltpu.VMEM((tb, d), F32), pltpu.VMEM((tb, d), F32),
                        pltpu.VMEM((tb, d), F32), pltpu.VMEM((tb, d), F32),
                        pltpu.VMEM((d, ff), BF16), pltpu.VMEM((d, ff), BF16), pltpu.VMEM((ff, d), BF16),
                        pltpu.SemaphoreType.DMA((2,)), pltpu.SemaphoreType.DMA((2,))],
    )
    return pl.pallas_call(
        _moe_kernel,
        grid_spec=grid_spec,
        out_shape=jax.ShapeDtypeStruct((n_out_rows, d), F32),
        compiler_params=_cparams(1),
        name="moe_experts",
    )(blk_e, n_used, idx, idx, w1, w3, w2, x)


def _moe_dispatch(route, *, n_experts, tb):
    t = route.shape[0]
    n_assign = TOP_K * t
    nb = n_assign // tb + n_experts + 1
    eid = route[:, :TOP_K].astype(jnp.int32).reshape(-1)
    assert n_experts * n_assign < 2 ** 31
    order = jnp.sort(eid * n_assign + jnp.arange(n_assign, dtype=jnp.int32)) % n_assign
    counts = jnp.sum((eid[:, None] == jnp.arange(n_experts, dtype=jnp.int32)[None, :]).astype(jnp.int32), axis=0)
    blocks = (counts + tb - 1) // tb
    bend = jnp.cumsum(blocks)
    start = jnp.cumsum(counts) - counts
    blk = jnp.arange(nb, dtype=jnp.int32)
    blk_e = jnp.minimum(jnp.sum((blk[:, None] >= bend[None, :]).astype(jnp.int32), axis=1), n_experts - 1)
    k_in_e = blk - (bend - blocks)[blk_e]
    n_valid = jnp.clip(counts[blk_e] - k_in_e * tb, 0, tb)
    r = jnp.arange(tb, dtype=jnp.int32)
    valid = r[None, :] < n_valid[:, None]
    src = jnp.where(valid, (start[blk_e] + k_in_e * tb)[:, None] + r[None, :], 0)
    a = order[src]
    a_tok = a // TOP_K
    tok = jnp.where(valid, a_tok, 0).reshape(-1)
    dst = jnp.where(valid, (a % TOP_K) * t + a_tok, TOP_K * t + r[None, :]).reshape(-1)
    return blk_e.astype(jnp.int32), bend[-1:].astype(jnp.int32), tok, dst


def _combine_ln_kernel(x_ref, y0_ref, y1_ref, r_ref, g_ref, b_ref, o_ref, *, alpha):
    w0 = r_ref[:, TOP_K:TOP_K + 1]
    w1 = r_ref[:, TOP_K + 1:TOP_K + 2]
    o_ref[...] = _layer_norm(alpha * x_ref[...] + w0 * y0_ref[...] + w1 * y1_ref[...], g_ref[...], b_ref[...])


def _combine_ln(x, y, route, g, b, *, alpha):
    t, d = x.shape
    tm = min(TM_PROJ, t)
    assert t % tm == 0
    nt = t // tm
    return pl.pallas_call(
        functools.partial(_combine_ln_kernel, alpha=alpha),
        grid=(nt,),
        in_specs=[pl.BlockSpec((tm, d), lambda i: (i, 0)),
                  pl.BlockSpec((tm, d), lambda i: (i, 0)),
                  pl.BlockSpec((tm, d), lambda i: (i + nt, 0)),
                  pl.BlockSpec((tm, LANES), lambda i: (i, 0)),
                  pl.BlockSpec((1, d), lambda i: (0, 0)), pl.BlockSpec((1, d), lambda i: (0, 0))],
        out_specs=pl.BlockSpec((tm, d), lambda i: (i, 0)),
        out_shape=jax.ShapeDtypeStruct((t, d), F32),
        compiler_params=_cparams(1),
        name="moe_combine_ln",
    )(x, y, y, route, g.reshape(1, d).astype(F32), b.reshape(1, d).astype(F32))


def _rope_tables(positions, dh):
    rope_dim = dh // 4
    half = rope_dim // 2
    assert half == ROPE_HALF and dh == LANES
    inv_freq = ROPE_THETA ** (-jnp.arange(0, rope_dim, 2, dtype=F32) / rope_dim)
    ang = positions.astype(F32).reshape(-1, 1) * inv_freq[None, :]
    cos, sin = jnp.cos(ang), jnp.sin(ang)
    t = ang.shape[0]
    c = jnp.concatenate([cos, cos, jnp.ones((t, dh - rope_dim), F32)], axis=1)
    s1 = jnp.concatenate([-sin, jnp.zeros((t, dh - half), F32)], axis=1)
    s2 = jnp.concatenate([jnp.zeros((t, half), F32), sin, jnp.zeros((t, dh - rope_dim), F32)], axis=1)
    return c, s1, s2


def kernel(x, mem, positions, w_in, w_out, ln_mix_g, ln_mix_b, ln_mem_g, ln_mem_b, ln_ffn_g, ln_ffn_b, lam_q1, lam_k1, lam_q2, lam_k2, diff_subln_g, conv_w, conv_b, conv_ln_g, conv_ln_b, sgu_ln_g, sgu_ln_b, sgu_w, sgu_b, sc_w, mem_kv_w, xq_w, xo_w, rg_w, rg_b, re_w, re_b, e_w1, e_w3, e_w2):
    bsz, seq, d = x.shape
    t = bsz * seq
    depth = w_in.shape[0]
    mem_len = mem.shape[1]
    half = w_in.shape[2] // 5
    dh = lam_q1.shape[-1]
    n_experts = re_w.shape[-1]
    alpha = (2 * depth) ** 0.25
    assert half % TN_PROJ == 0 and diff_subln_g.shape[-1] == 2 * dh and half == 2 * DIFF_HEADS * dh

    xt = x.reshape(t, d)
    rope = _rope_tables(positions, dh)
    kv = _inproj(mem.reshape(bsz * mem_len, d), mem_kv_w.astype(BF16))

    for l in range(depth):
        j = l // 2
        w_l = w_in[l].astype(BF16)
        if l % 2 == 0:
            h = _inproj(xt, w_l, rope, n_rope_tiles=2 * half // TN_PROJ, n_scale_tiles=half // TN_PROJ,
                        scale=dh ** -0.5 * math.log2(math.e))
            lam_init = 0.8 - 0.6 * math.exp(-0.3 * l)
            lam = (jnp.exp(jnp.sum(lam_q1[j].astype(F32) * lam_k1[j].astype(F32)))
                   - jnp.exp(jnp.sum(lam_q2[j].astype(F32) * lam_k2[j].astype(F32))) + lam_init)
            o = _diff_attention(h, lam, diff_subln_g[j], bsz=bsz, seq=seq, out_scale=1.0 - lam_init)
            c = _conformer(h, conv_w[j], conv_b[j], conv_ln_g[j], conv_ln_b[j], seq=seq, col_a=3, col_g=4)
            parts = [o, c]
        else:
            h = _inproj(xt, w_l)
            parts = [_sgu_shortconv(h, sgu_ln_g[j], sgu_ln_b[j], sgu_w[j], sgu_b[j], sc_w[j], seq=seq)]
        n_route = N_GROUPS + n_experts
        wr = jnp.concatenate([rg_w[l], re_w[l], jnp.zeros((d, LANES - n_route), F32)], axis=1).astype(BF16)
        br = jnp.concatenate([rg_b[l], re_b[l], jnp.zeros((LANES - n_route,), F32)]).reshape(1, LANES).astype(F32)
        xt, route = _mix_cross_router(parts, w_out[l].astype(BF16), xt, ln_mix_g[l], ln_mix_b[l], kv,
                                      xq_w[l].astype(BF16), xo_w[l].astype(BF16), ln_mem_g[l], ln_mem_b[l], wr, br,
                                      seq=seq, mem_len=mem_len, alpha=alpha, per_group=n_experts // N_GROUPS)

        blk_e, n_used, tok, dst = _moe_dispatch(route, n_experts=n_experts, tb=TB_MOE)
        y = _moe_experts(xt, blk_e, n_used, tok, dst, e_w1, e_w3, e_w2, layer=l, n_out_rows=TOP_K * t + TB_MOE)
        xt = _combine_ln(xt, y, route, ln_ffn_g[l], ln_ffn_b[l], alpha=alpha)
    return xt.reshape(bsz, seq, d)
```

```python
import functools
import math

import jax
import jax.numpy as jnp
from jax import lax
from jax.experimental import pallas as pl
from jax.experimental.pallas import tpu as pltpu

F32 = jnp.float32
BF16 = jnp.bfloat16

DIFF_HEADS = 4
ROPE_THETA = 500000.0
CONF_KERNEL = 31
SGU_CHUNK = 128
CROSS_HEADS = 4
N_GROUPS = 4
TOP_K = 2
LN_EPS = 1e-5

LANES = 128
SUBLANES = 8
NEG_BIG = -1e30
VMEM_LIMIT = 56 * 1024 * 1024

TM_PROJ = 512
TM_INPROJ = 1024
TN_PROJ = 1024
TQ_ATT = 512
ATT_RC = 64
TR_MIX = 256
CONV_HALO = 32
CONV_RB = 32
SC_HALO = 16
TB_MOE = 256


def _cparams(n_axes):
    return pltpu.CompilerParams(dimension_semantics=("arbitrary",) * n_axes,
                                vmem_limit_bytes=VMEM_LIMIT)


def _layer_norm(y, g, b):
    mu = jnp.mean(y, axis=-1, keepdims=True)
    d = y - mu
    var = jnp.mean(d * d, axis=-1, keepdims=True)
    return d * lax.rsqrt(var + LN_EPS) * g + b


def _inproj_kernel(x_ref, w_ref, *rest, n_rope_tiles, n_scale_tiles, scale):
    if n_rope_tiles:
        c_ref, s_ref, o_ref, xb_ref = rest
    else:
        o_ref, xb_ref = rest
    j = pl.program_id(1)

    @pl.when(j == 0)
    def _():
        xb_ref[...] = x_ref[...].astype(BF16)

    acc = jnp.dot(xb_ref[...], w_ref[...], preferred_element_type=F32)
    if not n_rope_tiles:
        o_ref[...] = acc.astype(o_ref.dtype)
        return

    @pl.when(j >= n_rope_tiles)
    def _():
        o_ref[...] = acc.astype(o_ref.dtype)

    @pl.when(j < n_rope_tiles)
    def _():
        a = acc * jnp.where(j < n_scale_tiles, scale, 1.0).astype(F32)
        c, s = c_ref[...], s_ref[...]
        for g in range(a.shape[1] // LANES):
            ag = a[:, g * LANES:(g + 1) * LANES]
            og = ag * c + pltpu.roll(ag, LANES // 2, axis=1) * s
            o_ref[:, g * LANES:(g + 1) * LANES] = og.astype(o_ref.dtype)


def _inproj(x, w_bf16, rope=None, *, n_rope_tiles=0, n_scale_tiles=0, scale=1.0, out_dtype=BF16):
    m, k = x.shape
    n = w_bf16.shape[1]
    tm = min(TM_INPROJ, m)
    tn = min(TN_PROJ, n)
    assert m % tm == 0 and n % tn == 0
    in_specs = [pl.BlockSpec((tm, k), lambda i, j: (i, 0)),
                pl.BlockSpec((k, tn), lambda i, j: (0, j))]
    args = [x, w_bf16]
    if n_rope_tiles:
        in_specs += [pl.BlockSpec((tm, LANES), lambda i, j: (i, 0))] * 2
        args += list(rope)
    return pl.pallas_call(
        functools.partial(_inproj_kernel, n_rope_tiles=n_rope_tiles, n_scale_tiles=n_scale_tiles, scale=scale),
        grid=(m // tm, n // tn),
        in_specs=in_specs,
        out_specs=pl.BlockSpec((tm, tn), lambda i, j: (i, j)),
        out_shape=jax.ShapeDtypeStruct((m, n), out_dtype),
        scratch_shapes=[pltpu.VMEM((tm, k), BF16)],
        compiler_params=_cparams(2),
        name="inproj_rope" if n_rope_tiles else "inproj",
    )(*args)


def _diff_attn_kernel(lam_ref, q_ref, k_ref, v_ref, g_ref, o_ref, s_a, s_b, x_a, x_b, p_a, p_b, a_a, a_b,
                      m_ref, l_ref, acc_ref, *, dh, out_scale):
    qi = pl.program_id(2)
    tq = q_ref.shape[0]
    tk = tq
    dv = v_ref.shape[1]
    ngrp = tk // LANES
    m_ref[...] = jnp.full(m_ref.shape, NEG_BIG, F32)
    l_ref[...] = jnp.zeros(l_ref.shape, F32)
    acc_ref[...] = jnp.zeros(acc_ref.shape, F32)

    def scores(j, s_ref, x_ref):
        r0 = pl.multiple_of(j * tk, tk)
        for mi in range(2):
            cols = slice(mi * dh, (mi + 1) * dh)
            s = lax.dot_general(q_ref[:, cols], k_ref[pl.ds(r0, tk), cols], (((1,), (1,)), ((), ())),
                                preferred_element_type=F32)
            s_ref[mi] = s
            mx = s[:, :LANES]
            for g in range(1, ngrp):
                mx = jnp.maximum(mx, s[:, g * LANES:(g + 1) * LANES])
            x_ref[mi] = mx

    def softmax(s_ref, x_ref, p_ref, a_ref, masked):
        for mi in range(2):
            for rc in range(tq // ATT_RC):
                rows = slice(rc * ATT_RC, (rc + 1) * ATT_RC)
                groups = [s_ref[mi, rows, g * LANES:(g + 1) * LANES] for g in range(ngrp)]
                if masked:
                    row = lax.broadcasted_iota(jnp.int32, (ATT_RC, LANES), 0) + rc * ATT_RC
                    col = lax.broadcasted_iota(jnp.int32, (ATT_RC, LANES), 1)
                    groups = [jnp.where(col + g * LANES <= row, sg, NEG_BIG) for g, sg in enumerate(groups)]
                    mx = functools.reduce(jnp.maximum, groups)
                else:
                    mx = x_ref[mi, rows]
                m_prev = m_ref[mi, rows]
                m_new = jnp.maximum(m_prev, jnp.max(mx, axis=1, keepdims=True))
                alpha = jnp.exp2(m_prev - m_new)
                lsum = None
                for g, sg in enumerate(groups):
                    pg = jnp.exp2((sg - m_new).astype(BF16))
                    lsum = pg if lsum is None else lsum + pg
                    p_ref[mi, rows, g * LANES:(g + 1) * LANES] = pg
                l_ref[mi, rows] = alpha * l_ref[mi, rows] + lsum.astype(F32)
                m_ref[mi, rows] = m_new
                a_ref[mi, rows] = alpha

    def weighted_values(j, p_ref, a_ref):
        r0 = pl.multiple_of(j * tk, tk)
        v = v_ref[pl.ds(r0, tk), :]
        for mi in range(2):
            pv = jnp.dot(p_ref[mi], v, preferred_element_type=F32)
            alpha = a_ref[mi]
            acc_ref[mi] = jnp.concatenate(
                [acc_ref[mi, :, c * LANES:(c + 1) * LANES] * alpha for c in range(dv // LANES)], axis=1) + pv

    scores(0, s_a, x_a)

    def pair(u, carry):
        j = 2 * u
        scores(j + 1, s_b, x_b)
        softmax(s_a, x_a, p_a, a_a, False)
        weighted_values(j, p_a, a_a)
        scores(j + 2, s_a, x_a)
        softmax(s_b, x_b, p_b, a_b, False)
        weighted_values(j + 1, p_b, a_b)
        return carry

    lax.fori_loop(0, lax.shift_right_logical(qi, 1), pair, 0)

    @pl.when((qi & 1) == 0)
    def _():
        softmax(s_a, x_a, p_a, a_a, True)
        weighted_values(qi, p_a, a_a)

    @pl.when((qi & 1) == 1)
    def _():
        scores(qi, s_b, x_b)
        softmax(s_a, x_a, p_a, a_a, False)
        weighted_values(qi - 1, p_a, a_a)
        softmax(s_b, x_b, p_b, a_b, True)
        weighted_values(qi, p_b, a_b)

    lam = lam_ref[0, 0]
    l0 = jnp.sum(l_ref[0], axis=1, keepdims=True)
    l1 = jnp.sum(l_ref[1], axis=1, keepdims=True)
    o = acc_ref[0] / l0 - lam * (acc_ref[1] / l1)
    ms = jnp.mean(o * o, axis=-1, keepdims=True)
    o_ref[...] = (o * lax.rsqrt(ms + LN_EPS) * g_ref[...] * out_scale).astype(o_ref.dtype)


def _diff_attention(h, lam, subln_g, *, bsz, seq, out_scale):
    t = h.shape[0]
    dv = subln_g.shape[-1]
    dh = dv // 2
    tq = min(TQ_ATT, seq)
    nq = seq // tq
    assert seq % tq == 0
    hd_n = DIFF_HEADS
    return pl.pallas_call(
        functools.partial(_diff_attn_kernel, dh=dh, out_scale=out_scale),
        grid=(bsz, hd_n, nq),
        in_specs=[
            pl.BlockSpec(memory_space=pltpu.SMEM),
            pl.BlockSpec((tq, dv), lambda b, hd, qi: (b * nq + qi, hd)),
            pl.BlockSpec((seq, dv), lambda b, hd, qi: (b, hd_n + hd)),
            pl.BlockSpec((seq, dv), lambda b, hd, qi: (b, 2 * hd_n + hd)),
            pl.BlockSpec((1, dv), lambda b, hd, qi: (0, 0)),
        ],
        out_specs=pl.BlockSpec((tq, dv), lambda b, hd, qi: (b * nq + qi, hd)),
        out_shape=jax.ShapeDtypeStruct((t, hd_n * dv), BF16),
        scratch_shapes=[pltpu.VMEM((2, tq, tq), F32), pltpu.VMEM((2, tq, tq), F32),
                        pltpu.VMEM((2, tq, LANES), F32), pltpu.VMEM((2, tq, LANES), F32),
                        pltpu.VMEM((2, tq, tq), BF16), pltpu.VMEM((2, tq, tq), BF16),
                        pltpu.VMEM((2, tq, LANES), F32), pltpu.VMEM((2, tq, LANES), F32),
                        pltpu.VMEM((2, tq, LANES), F32), pltpu.VMEM((2, tq, LANES), F32),
                        pltpu.VMEM((2, tq, dv), F32)],
        compiler_params=_cparams(3),
        name="diff_attn",
    )(lam.reshape(1, 1).astype(F32), h, h, h, subln_g.reshape(1, dv).astype(F32))


def _conformer_kernel(a_ref, g_ref, ah_ref, gh_ref, w_ref, cb_ref, lg_ref, lb_ref, o_ref, cext_ref, conv_ref,
                      *, tiles_per_seq):
    i = pl.program_id(0)
    tr, width = a_ref.shape
    halo = ah_ref.shape[0]
    ksz = w_ref.shape[0]
    first = (i % tiles_per_seq) == 0

    glu_h = ah_ref[...].astype(F32) * jax.nn.sigmoid(gh_ref[...].astype(F32))
    cext_ref[0:halo, :] = jnp.where(first, 0.0, glu_h)
    cext_ref[halo:halo + tr, :] = a_ref[...].astype(F32) * jax.nn.sigmoid(g_ref[...].astype(F32))
    cext_ref[halo + tr:halo + tr + SUBLANES, :] = jnp.zeros((SUBLANES, width), F32)

    base = halo - (ksz - 1)
    win = CONV_RB + halo + SUBLANES

    def chunk(rc, carry):
        r0 = pl.multiple_of(rc * CONV_RB, CONV_RB)
        for c in range(width // LANES):
            lanes = slice(c * LANES, (c + 1) * LANES)
            wnd = cext_ref[pl.ds(r0, win), lanes]
            acc = jnp.zeros((CONV_RB, LANES), F32)
            for b in range(SUBLANES):
                shifted = wnd if b == 0 else pltpu.roll(wnd, win - b, axis=0)
                for a in range((halo + SUBLANES) // SUBLANES):
                    j = SUBLANES * a + b - base
                    if 0 <= j < ksz:
                        acc = acc + w_ref[j:j + 1, lanes] * shifted[SUBLANES * a:SUBLANES * a + CONV_RB]
            conv_ref[pl.ds(r0, CONV_RB), lanes] = acc + cb_ref[:, lanes]
        return carry

    lax.fori_loop(0, tr // CONV_RB, chunk, 0)
    y = _layer_norm(conv_ref[...], lg_ref[...], lb_ref[...])
    o_ref[...] = (y * jax.nn.sigmoid(y)).astype(o_ref.dtype)


def _conformer(h, conv_w, conv_b, ln_g, ln_b, *, seq, col_a, col_g):
    t = h.shape[0]
    ksz, width = conv_w.shape
    tr = min(TR_MIX, seq)
    halo = CONV_HALO
    assert seq % tr == 0 and tr % halo == 0 and ksz - 1 <= halo and tr % CONV_RB == 0
    rpb = tr // halo
    row = lambda v: v.reshape(1, width).astype(F32)
    return pl.pallas_call(
        functools.partial(_conformer_kernel, tiles_per_seq=seq // tr),
        grid=(t // tr,),
        in_specs=[
            pl.BlockSpec((tr, width), lambda i: (i, col_a)),
            pl.BlockSpec((tr, width), lambda i: (i, col_g)),
            pl.BlockSpec((halo, width), lambda i: (jnp.maximum(i * rpb - 1, 0), col_a)),
            pl.BlockSpec((halo, width), lambda i: (jnp.maximum(i * rpb - 1, 0), col_g)),
            pl.BlockSpec((ksz, width), lambda i: (0, 0)),
            pl.BlockSpec((1, width), lambda i: (0, 0)),
            pl.BlockSpec((1, width), lambda i: (0, 0)),
            pl.BlockSpec((1, width), lambda i: (0, 0)),
        ],
        out_specs=pl.BlockSpec((tr, width), lambda i: (i, 0)),
        out_shape=jax.ShapeDtypeStruct((t, width), BF16),
        scratch_shapes=[pltpu.VMEM((halo + tr + SUBLANES, width), F32), pltpu.VMEM((tr, width), F32)],
        compiler_params=_cparams(1),
        name="conformer_conv",
    )(h, h, h, h, conv_w.astype(F32), row(conv_b), row(ln_g), row(ln_b))


def _gelu_exact(x):
    return 0.5 * x * (1.0 + lax.erf(x * math.sqrt(0.5)))


def _sgu_kernel(u_ref, v_ref, gb_ref, gc_ref, xi_ref, gch_ref, xih_ref, lg_ref, lb_ref, sw_ref, sb_ref, cw_ref,
                o_ref, pext_ref, *, tiles_per_seq):
    i = pl.program_id(0)
    tr, width = u_ref.shape
    halo = gch_ref.shape[0]
    n_grp, chunk, _ = sw_ref.shape
    gdim = width // n_grp
    first = (i % tiles_per_seq) == 0

    vg = _layer_norm(_gelu_exact(v_ref[...].astype(F32)), lg_ref[...], lb_ref[...]).astype(BF16)
    trow = lax.broadcasted_iota(jnp.int32, (chunk, chunk), 0)
    tcol = lax.broadcasted_iota(jnp.int32, (chunk, chunk), 1)
    for g in range(n_grp):
        wg = jnp.where(tcol <= trow, sw_ref[g], 0.0).astype(BF16)
        for n in range(tr // chunk):
            rows = slice(n * chunk, (n + 1) * chunk)
            cols = slice(g * gdim, (g + 1) * gdim)
            sv = jnp.dot(wg, vg[rows, cols], preferred_element_type=F32) + sb_ref[g]
            o_ref[rows, cols] = (_gelu_exact(u_ref[rows, cols].astype(F32)) * sv).astype(o_ref.dtype)

    ph = gch_ref[...].astype(F32) * xih_ref[...].astype(F32)
    pext_ref[0:halo, :] = jnp.where(first, 0.0, ph)
    pext_ref[halo:halo + tr, :] = gc_ref[...].astype(F32) * xi_ref[...].astype(F32)
    pe = pext_ref[...]
    n_ext = halo + tr
    ksz = cw_ref.shape[0]
    conv = cw_ref[ksz - 1:ksz, :] * pe[halo:]
    for back in range(1, ksz):
        conv = conv + cw_ref[ksz - 1 - back:ksz - back, :] * pltpu.roll(pe, back, axis=0)[halo:]
    del n_ext
    o_ref[:, width:2 * width] = (gb_ref[...].astype(F32) * conv).astype(o_ref.dtype)


def _sgu_shortconv(h, ln_g, ln_b, sgu_w, sgu_b, sc_w, *, seq):
    t = h.shape[0]
    width = ln_g.shape[-1]
    n_grp, chunk, _ = sgu_w.shape
    gdim = width // n_grp
    tr = min(TR_MIX, seq)
    halo = SC_HALO
    assert seq % tr == 0 and tr % chunk == 0 and tr % halo == 0 and sc_w.shape[0] - 1 <= halo
    rpb = tr // halo
    row = lambda v: v.reshape(1, width).astype(F32)
    bias = jnp.broadcast_to(sgu_b.astype(F32)[:, :, None], (n_grp, chunk, gdim))
    cur = lambda c: pl.BlockSpec((tr, width), lambda i: (i, c))
    prev = lambda c: pl.BlockSpec((halo, width), lambda i: (jnp.maximum(i * rpb - 1, 0), c))
    whole = lambda a: pl.BlockSpec(a.shape, lambda i: (0,) * a.ndim)
    small = [row(ln_g), row(ln_b), sgu_w.astype(F32), bias, sc_w.astype(F32)]
    return pl.pallas_call(
        functools.partial(_sgu_kernel, tiles_per_seq=seq // tr),
        grid=(t // tr,),
        in_specs=[cur(0), cur(1), cur(2), cur(3), cur(4), prev(3), prev(4)] + [whole(a) for a in small],
        out_specs=pl.BlockSpec((tr, 2 * width), lambda i: (i, 0)),
        out_shape=jax.ShapeDtypeStruct((t, 2 * width), BF16),
        scratch_shapes=[pltpu.VMEM((halo + tr, width), F32)],
        compiler_params=_cparams(1),
        name="sgu_shortconv",
    )(h, h, h, h, h, h, h, *small)


def _mix_cross_router_kernel(*refs, n_parts, alpha, scale, n_heads, n_groups, per_group):
    parts = refs[:n_parts]
    ws = refs[n_parts:2 * n_parts]
    x_ref, g1_ref, b1_ref = refs[2 * n_parts:2 * n_parts + 3]
    acc = alpha * x_ref[...]
    for p_ref, w_ref in zip(parts, ws):
        acc = acc + jnp.dot(p_ref[...], w_ref[...], preferred_element_type=F32)
    x1 = _layer_norm(acc, g1_ref[...], b1_ref[...])
    _cross_route(x1, *refs[2 * n_parts + 3:], alpha=alpha, scale=scale, n_heads=n_heads, n_groups=n_groups,
                 per_group=per_group)


def _cross_route(x, wq_ref, k_ref, v_ref, wo_ref, g_ref, b_ref, wr_ref, br_ref, o_ref, r_ref,
                 *, alpha, scale, n_heads, n_groups, per_group):
    q = (jnp.dot(x.astype(BF16), wq_ref[...], preferred_element_type=F32) * scale).astype(BF16)
    dh = q.shape[1] // n_heads
    outs = []
    for hh in range(n_heads):
        cols = slice(hh * dh, (hh + 1) * dh)
        s = lax.dot_general(q[:, cols], k_ref[:, cols], (((1,), (1,)), ((), ())), preferred_element_type=F32)
        e = jnp.exp(s - jnp.max(s, axis=-1, keepdims=True))
        pr = e / jnp.sum(e, axis=-1, keepdims=True)
        outs.append(jnp.dot(pr.astype(BF16), v_ref[:, cols], preferred_element_type=F32))
    o = jnp.concatenate(outs, axis=-1).astype(BF16)
    y = _layer_norm(alpha * x + jnp.dot(o, wo_ref[...], preferred_element_type=F32), g_ref[...], b_ref[...])
    o_ref[...] = y

    logits = jnp.dot(y.astype(BF16), wr_ref[...], preferred_element_type=F32) + br_ref[...]
    lane = lax.broadcasted_iota(jnp.int32, logits.shape, 1)
    far = jnp.int32(LANES)
    is_g = lane < n_groups
    gl = jnp.where(is_g, logits, -jnp.inf)
    gmax = jnp.max(gl, axis=-1, keepdims=True)
    grp = jnp.min(jnp.where(is_g & (gl == gmax), lane, far), axis=-1, keepdims=True)
    g_gate = 1.0 / jnp.sum(jnp.where(is_g, jnp.exp(gl - gmax), 0.0), axis=-1, keepdims=True)
    lo = n_groups + per_group * grp
    is_e = (lane >= lo) & (lane < lo + per_group)
    el = jnp.where(is_e, logits, -jnp.inf)
    v1 = jnp.max(el, axis=-1, keepdims=True)
    i1 = jnp.min(jnp.where(is_e & (el == v1), lane, far), axis=-1, keepdims=True)
    is_e2 = is_e & (lane != i1)
    el2 = jnp.where(is_e2, logits, -jnp.inf)
    v2 = jnp.max(el2, axis=-1, keepdims=True)
    i2 = jnp.min(jnp.where(is_e2 & (el2 == v2), lane, far), axis=-1, keepdims=True)
    e2 = jnp.exp(v2 - v1)
    w1 = g_gate / (1.0 + e2)
    w2 = g_gate * e2 / (1.0 + e2)
    r = jnp.where(lane == 0, (i1 - n_groups).astype(F32),
                  jnp.where(lane == 1, (i2 - n_groups).astype(F32),
                            jnp.where(lane == 2, w1, jnp.where(lane == 3, w2, 0.0))))
    r_ref[...] = r


def _mix_cross_router(parts, w_out_bf16, x, g1, b1, kv_bf16, wq_bf16, wo_bf16, g2, b2, wr_bf16, br,
                      *, seq, mem_len, alpha, per_group):
    t, d = x.shape
    cw = wq_bf16.shape[1]
    tm = min(TM_PROJ, seq)
    assert seq % tm == 0
    tps = seq // tm
    dh = cw // CROSS_HEADS
    n_exp = wr_bf16.shape[1]
    const = lambda shape: pl.BlockSpec(shape, lambda i: (0, 0), pipeline_mode=pl.Buffered(1))
    row = lambda v: v.reshape(1, d).astype(F32)
    in_specs, ws, off = [], [], 0
    for p in parts:
        kp = p.shape[1]
        in_specs.append(pl.BlockSpec((tm, kp), lambda i: (i, 0)))
        ws.append(w_out_bf16[off:off + kp])
        off += kp
    assert off == w_out_bf16.shape[0]
    in_specs += [const(w.shape) for w in ws]
    in_specs += [
        pl.BlockSpec((tm, d), lambda i: (i, 0)), const((1, d)), const((1, d)),
        const((d, cw)),
        pl.BlockSpec((mem_len, cw), lambda i: (i // tps, 0)),
        pl.BlockSpec((mem_len, cw), lambda i: (i // tps, 1)),
        const((cw, d)), const((1, d)), const((1, d)), const((d, n_exp)), const((1, n_exp)),
    ]
    return pl.pallas_call(
        functools.partial(_mix_cross_router_kernel, n_parts=len(parts), alpha=alpha, scale=dh ** -0.5,
                          n_heads=CROSS_HEADS, n_groups=N_GROUPS, per_group=per_group),
        grid=(t // tm,),
        in_specs=in_specs,
        out_specs=[pl.BlockSpec((tm, d), lambda i: (i, 0)), pl.BlockSpec((tm, LANES), lambda i: (i, 0))],
        out_shape=[jax.ShapeDtypeStruct((t, d), F32), jax.ShapeDtypeStruct((t, LANES), F32)],
        compiler_params=_cparams(1),
        name="mix_cross_router",
    )(*parts, *ws, x, row(g1), row(b1), wq_bf16, kv_bf16, kv_bf16, wo_bf16, row(g2), row(b2), wr_bf16, br)


def _moe_kernel(be_ref, nu_ref, idx0_ref, idx_ref, w1_ref, w3_ref, w2_ref,
                x_hbm, o_hbm, xb0, xb1, yb0, yb1, wb1, wb3, wb2, gsem, ssem):
    i = pl.program_id(0)
    n_used = nu_ref[0]
    tb = xb0.shape[0]
    xbufs, ybufs = (xb0, xb1), (yb0, yb1)

    def gather_row(tok, r, s):
        return pltpu.make_async_copy(x_hbm.at[pl.ds(tok, 1), :], xbufs[s].at[pl.ds(r, 1), :], gsem.at[s])

    def scatter_row(dst, r, s):
        return pltpu.make_async_copy(ybufs[s].at[pl.ds(r, 1), :], o_hbm.at[pl.ds(dst, 1), :], ssem.at[s])

    def wait_gather(s):
        pltpu.make_async_copy(x_hbm.at[pl.ds(0, tb), :], xbufs[s], gsem.at[s]).wait()

    def wait_scatter(s):
        pltpu.make_async_copy(ybufs[s], o_hbm.at[pl.ds(0, tb), :], ssem.at[s]).wait()

    def step(slot):
        other = 1 - slot

        @pl.when(i == 0)
        def _():
            def body(r, c):
                gather_row(idx0_ref[0, 0, r], r, 0).start()
                return c
            lax.fori_loop(0, tb, body, 0, unroll=8)
            yb1[...] = jnp.zeros(yb1.shape, F32)

        first_of_expert = jnp.logical_or(i == 0, be_ref[i] != be_ref[jnp.maximum(i - 1, 0)])

        @pl.when(first_of_expert)
        def _():
            wb1[...] = w1_ref[0, 0].astype(BF16)
            wb3[...] = w3_ref[0, 0].astype(BF16)
            wb2[...] = w2_ref[0, 0].astype(BF16)

        wait_gather(slot)
        for r in range(tb):
            gather_row(idx_ref[0, 0, r], r, other).start(priority=r % 2)
        for r in range(tb):
            scatter_row(idx_ref[0, 0, tb + r], r, other).start(priority=r % 2)
        xb = xbufs[slot][...].astype(BF16)
        h1 = jnp.dot(xb, wb1[...], preferred_element_type=F32)
        h3 = jnp.dot(xb, wb3[...], preferred_element_type=F32)
        hb = (h1 * jax.nn.sigmoid(h1) * h3).astype(BF16)
        ybufs[slot][...] = jnp.dot(hb, wb2[...], preferred_element_type=F32)
        wait_scatter(other)

    def drain(slot):
        other = 1 - slot
        wait_gather(slot)

        def body(r, c):
            scatter_row(idx_ref[0, 0, tb + r], r, other).start()
            return c
        lax.fori_loop(0, tb, body, 0, unroll=8)
        wait_scatter(other)

    for parity in range(2):
        @pl.when(jnp.logical_and(i < n_used, (i & 1) == parity))
        def _():
            step(parity)

        @pl.when(jnp.logical_and(i == n_used, (i & 1) == parity))
        def _():
            drain(parity)


def _moe_experts(x, blk_e, n_used, tok, dst, w1, w3, w2, *, layer, n_out_rows):
    t, d = x.shape
    nb = blk_e.shape[0]
    tb = tok.shape[0] // nb
    ff = w1.shape[-1]
    idx = jnp.concatenate([tok.reshape(nb, 1, tb), jnp.roll(dst.reshape(nb, 1, tb), 2, axis=0)], axis=-1)
    smem_blk = lambda f: pl.BlockSpec((1, 1, 2 * tb), f, memory_space=pltpu.SMEM)
    grid_spec = pltpu.PrefetchScalarGridSpec(
        num_scalar_prefetch=2,
        grid=(nb,),
        in_specs=[
            smem_blk(lambda i, be, nu: (0, 0, 0)),
            smem_blk(lambda i, be, nu: (jnp.minimum(i + 1, nb - 1), 0, 0)),
            pl.BlockSpec((1, 1, d, ff), lambda i, be, nu: (layer, be[i], 0, 0)),
            pl.BlockSpec((1, 1, d, ff), lambda i, be, nu: (layer, be[i], 0, 0)),
            pl.BlockSpec((1, 1, ff, d), lambda i, be, nu: (layer, be[i], 0, 0)),
            pl.BlockSpec(memory_space=pl.ANY),
        ],
        out_specs=pl.BlockSpec(memory_space=pl.ANY),
        scratch_shapes=[pltpu.VMEM((tb, d), F32), pltpu.VMEM((tb, d), F32),
                        pltpu.VMEM((tb, d), F32), pltpu.VMEM((tb, d), F32),
                        pltpu.VMEM((d, ff), BF16), pltpu.VMEM((d, ff), BF16), pltpu.VMEM((ff, d), BF16),
                        pltpu.SemaphoreType.DMA((2,)), pltpu.SemaphoreType.DMA((2,))],
    )
    return pl.pallas_call(
        _moe_kernel,
        grid_spec=grid_spec,
        out_shape=jax.ShapeDtypeStruct((n_out_rows, d), F32),
        compiler_params=_cparams(1),
        name="moe_experts",
    )(blk_e, n_used, idx, idx, w1, w3, w2, x)


def _moe_dispatch(route, *, n_experts, tb):
    t = route.shape[0]
    n_assign = TOP_K * t
    nb = n_assign // tb + n_experts + 1
    eid = route[:, :TOP_K].astype(jnp.int32).reshape(-1)
    assert n_experts * n_assign < 2 ** 31
    order = jnp.sort(eid * n_assign + jnp.arange(n_assign, dtype=jnp.int32)) % n_assign
    counts = jnp.sum((eid[:, None] == jnp.arange(n_experts, dtype=jnp.int32)[None, :]).astype(jnp.int32), axis=0)
    blocks = (counts + tb - 1) // tb
    bend = jnp.cumsum(blocks)
    start = jnp.cumsum(counts) - counts
    blk = jnp.arange(nb, dtype=jnp.int32)
    blk_e = jnp.minimum(jnp.sum((blk[:, None] >= bend[None, :]).astype(jnp.int32), axis=1), n_experts - 1)
    k_in_e = blk - (bend - blocks)[blk_e]
    n_valid = jnp.clip(counts[blk_e] - k_in_e * tb, 0, tb)
    r = jnp.arange(tb, dtype=jnp.int32)
    valid = r[None, :] < n_valid[:, None]
    src = jnp.where(valid, (start[blk_e] + k_in_e * tb)[:, None] + r[None, :], 0)
    a = order[src]
    a_tok = a // TOP_K
    tok = jnp.where(valid, a_tok, 0).reshape(-1)
    dst = jnp.where(valid, (a % TOP_K) * t + a_tok, TOP_K * t + r[None, :]).reshape(-1)
    return blk_e.astype(jnp.int32), bend[-1:].astype(jnp.int32), tok, dst


def _combine_ln_kernel(x_ref, y0_ref, y1_ref, r_ref, g_ref, b_ref, o_ref, *, alpha):
    w0 = r_ref[:, TOP_K:TOP_K + 1]
    w1 = r_ref[:, TOP_K + 1:TOP_K + 2]
    o_ref[...] = _layer_norm(alpha * x_ref[...] + w0 * y0_ref[...] + w1 * y1_ref[...], g_ref[...], b_ref[...])


def _combine_ln(x, y, route, g, b, *, alpha):
    t, d = x.shape
    tm = min(TM_PROJ, t)
    assert t % tm == 0
    nt = t // tm
    return pl.pallas_call(
        functools.partial(_combine_ln_kernel, alpha=alpha),
        grid=(nt,),
        in_specs=[pl.BlockSpec((tm, d), lambda i: (i, 0)),
                  pl.BlockSpec((tm, d), lambda i: (i, 0)),
                  pl.BlockSpec((tm, d), lambda i: (i + nt, 0)),
                  pl.BlockSpec((tm, LANES), lambda i: (i, 0)),
                  pl.BlockSpec((1, d), lambda i: (0, 0)), pl.BlockSpec((1, d), lambda i: (0, 0))],
        out_specs=pl.BlockSpec((tm, d), lambda i: (i, 0)),
        out_shape=jax.ShapeDtypeStruct((t, d), F32),
        compiler_params=_cparams(1),
        name="moe_combine_ln",
    )(x, y, y, route, g.reshape(1, d).astype(F32), b.reshape(1, d).astype(F32))


def _rope_head_perm(dh):
    r = dh // 8
    old = list(range(dh))
    rest = old[2 * r:]
    n_low = dh // 2 - r
    return jnp.asarray(old[:r] + rest[:n_low] + old[r:2 * r] + rest[n_low:], jnp.int32)


def _rope_tables(positions, dh):
    rope_dim = dh // 4
    half = rope_dim // 2
    assert dh == LANES
    inv_freq = ROPE_THETA ** (-jnp.arange(0, rope_dim, 2, dtype=F32) / rope_dim)
    ang = positions.astype(F32).reshape(-1, 1) * inv_freq[None, :]
    cos, sin = jnp.cos(ang), jnp.sin(ang)
    t = ang.shape[0]
    gap = dh // 2 - half
    c = jnp.concatenate([cos, jnp.ones((t, gap), F32), cos, jnp.ones((t, gap), F32)], axis=1)
    s = jnp.concatenate([-sin, jnp.zeros((t, gap), F32), sin, jnp.zeros((t, gap), F32)], axis=1)
    return c, s


def kernel(x, mem, positions, w_in, w_out, ln_mix_g, ln_mix_b, ln_mem_g, ln_mem_b, ln_ffn_g, ln_ffn_b, lam_q1, lam_k1, lam_q2, lam_k2, diff_subln_g, conv_w, conv_b, conv_ln_g, conv_ln_b, sgu_ln_g, sgu_ln_b, sgu_w, sgu_b, sc_w, mem_kv_w, xq_w, xo_w, rg_w, rg_b, re_w, re_b, e_w1, e_w3, e_w2):
    bsz, seq, d = x.shape
    t = bsz * seq
    depth = w_in.shape[0]
    mem_len = mem.shape[1]
    half = w_in.shape[2] // 5
    dh = lam_q1.shape[-1]
    n_experts = re_w.shape[-1]
    alpha = (2 * depth) ** 0.25
    assert half % TN_PROJ == 0 and diff_subln_g.shape[-1] == 2 * dh and half == 2 * DIFF_HEADS * dh

    xt = x.reshape(t, d)
    rope = _rope_tables(positions, dh)
    kv = _inproj(mem.reshape(bsz * mem_len, d), mem_kv_w.astype(BF16))

    for l in range(depth):
        j = l // 2
        w_l = w_in[l].astype(BF16)
        if l % 2 == 0:
            qk_cols = 2 * half
            col = jnp.arange(qk_cols, dtype=jnp.int32)
            col = (col // dh) * dh + _rope_head_perm(dh)[col % dh]
            w_l = jnp.concatenate([jnp.take(w_l[:, :qk_cols], col, axis=1), w_l[:, qk_cols:]], axis=1)
            h = _inproj(xt, w_l, rope, n_rope_tiles=2 * half // TN_PROJ, n_scale_tiles=half // TN_PROJ,
                        scale=dh ** -0.5 * math.log2(math.e))
            lam_init = 0.8 - 0.6 * math.exp(-0.3 * l)
            lam = (jnp.exp(jnp.sum(lam_q1[j].astype(F32) * lam_k1[j].astype(F32)))
                   - jnp.exp(jnp.sum(lam_q2[j].astype(F32) * lam_k2[j].astype(F32))) + lam_init)
            o = _diff_attention(h, lam, diff_subln_g[j], bsz=bsz, seq=seq, out_scale=1.0 - lam_init)
            c = _conformer(h, conv_w[j], conv_b[j], conv_ln_g[j], conv_ln_b[j], seq=seq, col_a=3, col_g=4)
            parts = [o, c]
        else:
            h = _inproj(xt, w_l)
            parts = [_sgu_shortconv(h, sgu_ln_g[j], sgu_ln_b[j], sgu_w[j], sgu_b[j], sc_w[j], seq=seq)]
        n_route = N_GROUPS + n_experts
        wr = jnp.concatenate([rg_w[l], re_w[l], jnp.zeros((d, LANES - n_route), F32)], axis=1).astype(BF16)
        br = jnp.concatenate([rg_b[l], re_b[l], jnp.zeros((LANES - n_route,), F32)]).reshape(1, LANES).astype(F32)
        xt, route = _mix_cross_router(parts, w_out[l].astype(BF16), xt, ln_mix_g[l], ln_mix_b[l], kv,
                                      xq_w[l].astype(BF16), xo_w[l].astype(BF16), ln_mem_g[l], ln_mem_b[l], wr, br,
                                      seq=seq, mem_len=mem_len, alpha=alpha, per_group=n_experts // N_GROUPS)

        blk_e, n_used, tok, dst = _moe_dispatch(route, n_experts=n_experts, tb=TB_MOE)
        y = _moe_experts(xt, blk_e, n_used, tok, dst, e_w1, e_w3, e_w2, layer=l, n_out_rows=TOP_K * t + TB_MOE)
        xt = _combine_ln(xt, y, route, ln_ffn_g[l], ln_ffn_b[l], alpha=alpha)
    return xt.reshape(bsz, seq, d)
```

```python
import functools
import math

import jax
import jax.numpy as jnp
from jax import lax
from jax.experimental import pallas as pl
from jax.experimental.pallas import tpu as pltpu

F32 = jnp.float32
BF16 = jnp.bfloat16

DIFF_HEADS = 4
ROPE_THETA = 500000.0
CONF_KERNEL = 31
SGU_CHUNK = 128
CROSS_HEADS = 4
N_GROUPS = 4
TOP_K = 2
LN_EPS = 1e-5

LANES = 128
SUBLANES = 8
NEG_BIG = -1e30
VMEM_LIMIT = 56 * 1024 * 1024

TM_PROJ = 512
TM_INPROJ = 1024
TN_PROJ = 1024
TQ_ATT = 1024
ATT_RC = 64
TR_MIX = 256
CONV_HALO = 32
CONV_RB = 32
SC_HALO = 16
TB_MOE = 256


def _cparams(n_axes):
    return pltpu.CompilerParams(dimension_semantics=("arbitrary",) * n_axes,
                                vmem_limit_bytes=VMEM_LIMIT)


def _layer_norm(y, g, b):
    mu = jnp.mean(y, axis=-1, keepdims=True)
    d = y - mu
    var = jnp.mean(d * d, axis=-1, keepdims=True)
    return d * lax.rsqrt(var + LN_EPS) * g + b


def _inproj_kernel(x_ref, w_ref, *rest, n_rope_tiles, n_scale_tiles, scale):
    if n_rope_tiles:
        c_ref, s_ref, o_ref, xb_ref = rest
    else:
        o_ref, xb_ref = rest
    j = pl.program_id(1)

    @pl.when(j == 0)
    def _():
        xb_ref[...] = x_ref[...].astype(BF16)

    acc = jnp.dot(xb_ref[...], w_ref[...], preferred_element_type=F32)
    if not n_rope_tiles:
        o_ref[...] = acc.astype(o_ref.dtype)
        return

    @pl.when(j >= n_rope_tiles)
    def _():
        o_ref[...] = acc.astype(o_ref.dtype)

    @pl.when(j < n_rope_tiles)
    def _():
        a = acc * jnp.where(j < n_scale_tiles, scale, 1.0).astype(F32)
        c, s = c_ref[...], s_ref[...]
        for g in range(a.shape[1] // LANES):
            ag = a[:, g * LANES:(g + 1) * LANES]
            og = ag * c + pltpu.roll(ag, LANES // 2, axis=1) * s
            o_ref[:, g * LANES:(g + 1) * LANES] = og.astype(o_ref.dtype)


def _inproj(x, w_bf16, rope=None, *, n_rope_tiles=0, n_scale_tiles=0, scale=1.0, out_dtype=BF16):
    m, k = x.shape
    n = w_bf16.shape[1]
    tm = min(TM_INPROJ, m)
    tn = min(TN_PROJ, n)
    assert m % tm == 0 and n % tn == 0
    in_specs = [pl.BlockSpec((tm, k), lambda i, j: (i, 0)),
                pl.BlockSpec((k, tn), lambda i, j: (0, j))]
    args = [x, w_bf16]
    if n_rope_tiles:
        in_specs += [pl.BlockSpec((tm, LANES), lambda i, j: (i, 0))] * 2
        args += list(rope)
    return pl.pallas_call(
        functools.partial(_inproj_kernel, n_rope_tiles=n_rope_tiles, n_scale_tiles=n_scale_tiles, scale=scale),
        grid=(m // tm, n // tn),
        in_specs=in_specs,
        out_specs=pl.BlockSpec((tm, tn), lambda i, j: (i, j)),
        out_shape=jax.ShapeDtypeStruct((m, n), out_dtype),
        scratch_shapes=[pltpu.VMEM((tm, k), BF16)],
        compiler_params=_cparams(2),
        name="inproj_rope" if n_rope_tiles else "inproj",
    )(*args)


def _diff_attn_kernel(lam_ref, q_ref, k_ref, v_ref, g_ref, o_ref, s_a, s_b, p_a, p_b, a_a, a_b,
                      m_ref, l_ref, acc_ref, *, dh, out_scale):
    qi = pl.program_id(2)
    tq = q_ref.shape[0]
    tk = s_a.shape[2]
    dv = v_ref.shape[1]
    ngrp = tk // LANES
    m_ref[...] = jnp.full(m_ref.shape, NEG_BIG, F32)
    l_ref[...] = jnp.zeros(l_ref.shape, F32)
    acc_ref[...] = jnp.zeros(acc_ref.shape, F32)

    def scores(j, s_ref):
        r0 = pl.multiple_of(j * tk, tk)
        for mi in range(2):
            cols = slice(mi * dh, (mi + 1) * dh)
            s_ref[mi] = lax.dot_general(q_ref[:, cols], k_ref[pl.ds(r0, tk), cols], (((1,), (1,)), ((), ())),
                                        preferred_element_type=F32)

    def softmax(s_ref, p_ref, a_ref, mask_shift=None):
        for mi in range(2):
            for rc in range(tq // ATT_RC):
                rows = slice(rc * ATT_RC, (rc + 1) * ATT_RC)
                groups = [s_ref[mi, rows, g * LANES:(g + 1) * LANES] for g in range(ngrp)]
                if mask_shift is not None:
                    row = lax.broadcasted_iota(jnp.int32, (ATT_RC, LANES), 0) + rc * ATT_RC
                    col = lax.broadcasted_iota(jnp.int32, (ATT_RC, LANES), 1) + mask_shift
                    groups = [jnp.where(col + g * LANES <= row, sg, NEG_BIG) for g, sg in enumerate(groups)]
                mx = functools.reduce(jnp.maximum, groups)
                m_prev = m_ref[mi, rows]
                m_new = jnp.maximum(m_prev, jnp.max(mx, axis=1, keepdims=True))
                alpha = jnp.exp2(m_prev - m_new)
                lsum = None
                for g, sg in enumerate(groups):
                    pg = jnp.exp2(sg - m_new)
                    lsum = pg if lsum is None else lsum + pg
                    p_ref[mi, rows, g * LANES:(g + 1) * LANES] = pg.astype(BF16)
                l_ref[mi, rows] = alpha * l_ref[mi, rows] + lsum
                m_ref[mi, rows] = m_new
                a_ref[mi, rows] = alpha

    def weighted_values(j, p_ref, a_ref):
        r0 = pl.multiple_of(j * tk, tk)
        v = v_ref[pl.ds(r0, tk), :]
        for mi in range(2):
            pv = jnp.dot(p_ref[mi], v, preferred_element_type=F32)
            alpha = a_ref[mi]
            acc_ref[mi] = jnp.concatenate(
                [acc_ref[mi, :, c * LANES:(c + 1) * LANES] * alpha for c in range(dv // LANES)], axis=1) + pv

    scores(0, s_a)

    def pair(u, carry):
        j = 2 * u
        scores(j + 1, s_b)
        softmax(s_a, p_a, a_a)
        weighted_values(j, p_a, a_a)
        scores(j + 2, s_a)
        softmax(s_b, p_b, a_b)
        weighted_values(j + 1, p_b, a_b)
        return carry

    lax.fori_loop(0, qi, pair, 0)
    scores(2 * qi + 1, s_b)
    softmax(s_a, p_a, a_a, mask_shift=0)
    weighted_values(2 * qi, p_a, a_a)
    softmax(s_b, p_b, a_b, mask_shift=tk)
    weighted_values(2 * qi + 1, p_b, a_b)

    lam = lam_ref[0, 0]
    l0 = jnp.sum(l_ref[0], axis=1, keepdims=True)
    l1 = jnp.sum(l_ref[1], axis=1, keepdims=True)
    o = acc_ref[0] / l0 - lam * (acc_ref[1] / l1)
    ms = jnp.mean(o * o, axis=-1, keepdims=True)
    o_ref[...] = (o * lax.rsqrt(ms + LN_EPS) * g_ref[...] * out_scale).astype(o_ref.dtype)


def _diff_attention(h, lam, subln_g, *, bsz, seq, out_scale):
    t = h.shape[0]
    dv = subln_g.shape[-1]
    dh = dv // 2
    tq = min(TQ_ATT, seq)
    tk = tq // 2
    nq = seq // tq
    assert seq % tq == 0 and tq % (2 * ATT_RC) == 0
    hd_n = DIFF_HEADS
    resident = lambda c0: pl.BlockSpec((seq, dv), lambda b, hd, qi: (b, c0 + hd), pipeline_mode=pl.Buffered(1))
    return pl.pallas_call(
        functools.partial(_diff_attn_kernel, dh=dh, out_scale=out_scale),
        grid=(bsz, hd_n, nq),
        in_specs=[
            pl.BlockSpec(memory_space=pltpu.SMEM),
            pl.BlockSpec((tq, dv), lambda b, hd, qi: (b * nq + qi, hd)),
            resident(hd_n), resident(2 * hd_n),
            pl.BlockSpec((1, dv), lambda b, hd, qi: (0, 0)),
        ],
        out_specs=pl.BlockSpec((tq, dv), lambda b, hd, qi: (b * nq + qi, hd)),
        out_shape=jax.ShapeDtypeStruct((t, hd_n * dv), BF16),
        scratch_shapes=[pltpu.VMEM((2, tq, tk), F32), pltpu.VMEM((2, tq, tk), F32),
                        pltpu.VMEM((2, tq, tk), BF16), pltpu.VMEM((2, tq, tk), BF16),
                        pltpu.VMEM((2, tq, LANES), F32), pltpu.VMEM((2, tq, LANES), F32),
                        pltpu.VMEM((2, tq, LANES), F32), pltpu.VMEM((2, tq, LANES), F32),
                        pltpu.VMEM((2, tq, dv), F32)],
        compiler_params=_cparams(3),
        name="diff_attn",
    )(lam.reshape(1, 1).astype(F32), h, h, h, subln_g.reshape(1, dv).astype(F32))


def _conformer_kernel(a_ref, g_ref, ah_ref, gh_ref, w_ref, cb_ref, lg_ref, lb_ref, o_ref, cext_ref, conv_ref,
                      *, tiles_per_seq):
    i = pl.program_id(0)
    tr, width = a_ref.shape
    halo = ah_ref.shape[0]
    ksz = w_ref.shape[0]
    first = (i % tiles_per_seq) == 0

    glu_h = ah_ref[...].astype(F32) * jax.nn.sigmoid(gh_ref[...].astype(F32))
    cext_ref[0:halo, :] = jnp.where(first, 0.0, glu_h)
    cext_ref[halo:halo + tr, :] = a_ref[...].astype(F32) * jax.nn.sigmoid(g_ref[...].astype(F32))
    cext_ref[halo + tr:halo + tr + SUBLANES, :] = jnp.zeros((SUBLANES, width), F32)

    base = halo - (ksz - 1)
    win = CONV_RB + halo + SUBLANES

    def chunk(rc, carry):
        r0 = pl.multiple_of(rc * CONV_RB, CONV_RB)
        for c in range(width // LANES):
            lanes = slice(c * LANES, (c + 1) * LANES)
            wnd = cext_ref[pl.ds(r0, win), lanes]
            acc = jnp.zeros((CONV_RB, LANES), F32)
            for b in range(SUBLANES):
                shifted = wnd if b == 0 else pltpu.roll(wnd, win - b, axis=0)
                for a in range((halo + SUBLANES) // SUBLANES):
                    j = SUBLANES * a + b - base
                    if 0 <= j < ksz:
                        acc = acc + w_ref[j:j + 1, lanes] * shifted[SUBLANES * a:SUBLANES * a + CONV_RB]
            conv_ref[pl.ds(r0, CONV_RB), lanes] = acc + cb_ref[:, lanes]
        return carry

    lax.fori_loop(0, tr // CONV_RB, chunk, 0)
    y = _layer_norm(conv_ref[...], lg_ref[...], lb_ref[...])
    o_ref[...] = (y * jax.nn.sigmoid(y)).astype(o_ref.dtype)


def _conformer(h, conv_w, conv_b, ln_g, ln_b, *, seq, col_a, col_g):
    t = h.shape[0]
    ksz, width = conv_w.shape
    tr = min(TR_MIX, seq)
    halo = CONV_HALO
    assert seq % tr == 0 and tr % halo == 0 and ksz - 1 <= halo and tr % CONV_RB == 0
    rpb = tr // halo
    row = lambda v: v.reshape(1, width).astype(F32)
    return pl.pallas_call(
        functools.partial(_conformer_kernel, tiles_per_seq=seq // tr),
        grid=(t // tr,),
        in_specs=[
            pl.BlockSpec((tr, width), lambda i: (i, col_a)),
            pl.BlockSpec((tr, width), lambda i: (i, col_g)),
            pl.BlockSpec((halo, width), lambda i: (jnp.maximum(i * rpb - 1, 0), col_a)),
            pl.BlockSpec((halo, width), lambda i: (jnp.maximum(i * rpb - 1, 0), col_g)),
            pl.BlockSpec((ksz, width), lambda i: (0, 0)),
            pl.BlockSpec((1, width), lambda i: (0, 0)),
            pl.BlockSpec((1, width), lambda i: (0, 0)),
            pl.BlockSpec((1, width), lambda i: (0, 0)),
        ],
        out_specs=pl.BlockSpec((tr, width), lambda i: (i, 0)),
        out_shape=jax.ShapeDtypeStruct((t, width), BF16),
        scratch_shapes=[pltpu.VMEM((halo + tr + SUBLANES, width), F32), pltpu.VMEM((tr, width), F32)],
        compiler_params=_cparams(1),
        name="conformer_conv",
    )(h, h, h, h, conv_w.astype(F32), row(conv_b), row(ln_g), row(ln_b))


def _gelu_exact(x):
    return 0.5 * x * (1.0 + lax.erf(x * math.sqrt(0.5)))


def _sgu_kernel(u_ref, v_ref, gb_ref, gc_ref, xi_ref, gch_ref, xih_ref, lg_ref, lb_ref, sw_ref, sb_ref, cw_ref,
                o_ref, pext_ref, *, tiles_per_seq):
    i = pl.program_id(0)
    tr, width = u_ref.shape
    halo = gch_ref.shape[0]
    n_grp, chunk, _ = sw_ref.shape
    gdim = width // n_grp
    first = (i % tiles_per_seq) == 0

    vg = _layer_norm(_gelu_exact(v_ref[...].astype(F32)), lg_ref[...], lb_ref[...]).astype(BF16)
    trow = lax.broadcasted_iota(jnp.int32, (chunk, chunk), 0)
    tcol = lax.broadcasted_iota(jnp.int32, (chunk, chunk), 1)
    for g in range(n_grp):
        wg = jnp.where(tcol <= trow, sw_ref[g], 0.0).astype(BF16)
        for n in range(tr // chunk):
            rows = slice(n * chunk, (n + 1) * chunk)
            cols = slice(g * gdim, (g + 1) * gdim)
            sv = jnp.dot(wg, vg[rows, cols], preferred_element_type=F32) + sb_ref[g]
            o_ref[rows, cols] = (_gelu_exact(u_ref[rows, cols].astype(F32)) * sv).astype(o_ref.dtype)

    ph = gch_ref[...].astype(F32) * xih_ref[...].astype(F32)
    pext_ref[0:halo, :] = jnp.where(first, 0.0, ph)
    pext_ref[halo:halo + tr, :] = gc_ref[...].astype(F32) * xi_ref[...].astype(F32)
    pe = pext_ref[...]
    n_ext = halo + tr
    ksz = cw_ref.shape[0]
    conv = cw_ref[ksz - 1:ksz, :] * pe[halo:]
    for back in range(1, ksz):
        conv = conv + cw_ref[ksz - 1 - back:ksz - back, :] * pltpu.roll(pe, back, axis=0)[halo:]
    del n_ext
    o_ref[:, width:2 * width] = (gb_ref[...].astype(F32) * conv).astype(o_ref.dtype)


def _sgu_shortconv(h, ln_g, ln_b, sgu_w, sgu_b, sc_w, *, seq):
    t = h.shape[0]
    width = ln_g.shape[-1]
    n_grp, chunk, _ = sgu_w.shape
    gdim = width // n_grp
    tr = min(TR_MIX, seq)
    halo = SC_HALO
    assert seq % tr == 0 and tr % chunk == 0 and tr % halo == 0 and sc_w.shape[0] - 1 <= halo
    rpb = tr // halo
    row = lambda v: v.reshape(1, width).astype(F32)
    bias = jnp.broadcast_to(sgu_b.astype(F32)[:, :, None], (n_grp, chunk, gdim))
    cur = lambda c: pl.BlockSpec((tr, width), lambda i: (i, c))
    prev = lambda c: pl.BlockSpec((halo, width), lambda i: (jnp.maximum(i * rpb - 1, 0), c))
    whole = lambda a: pl.BlockSpec(a.shape, lambda i: (0,) * a.ndim)
    small = [row(ln_g), row(ln_b), sgu_w.astype(F32), bias, sc_w.astype(F32)]
    return pl.pallas_call(
        functools.partial(_sgu_kernel, tiles_per_seq=seq // tr),
        grid=(t // tr,),
        in_specs=[cur(0), cur(1), cur(2), cur(3), cur(4), prev(3), prev(4)] + [whole(a) for a in small],
        out_specs=pl.BlockSpec((tr, 2 * width), lambda i: (i, 0)),
        out_shape=jax.ShapeDtypeStruct((t, 2 * width), BF16),
        scratch_shapes=[pltpu.VMEM((halo + tr, width), F32)],
        compiler_params=_cparams(1),
        name="sgu_shortconv",
    )(h, h, h, h, h, h, h, *small)


def _mix_cross_router_kernel(*refs, n_parts, alpha, scale, n_heads, n_groups, per_group):
    parts = refs[:n_parts]
    ws = refs[n_parts:2 * n_parts]
    x_ref, g1_ref, b1_ref = refs[2 * n_parts:2 * n_parts + 3]
    acc = alpha * x_ref[...]
    for p_ref, w_ref in zip(parts, ws):
        acc = acc + jnp.dot(p_ref[...], w_ref[...], preferred_element_type=F32)
    x1 = _layer_norm(acc, g1_ref[...], b1_ref[...])
    _cross_route(x1, *refs[2 * n_parts + 3:], alpha=alpha, scale=scale, n_heads=n_heads, n_groups=n_groups,
                 per_group=per_group)


def _cross_route(x, wq_ref, k_ref, v_ref, wo_ref, g_ref, b_ref, wr_ref, br_ref, o_ref, r_ref,
                 *, alpha, scale, n_heads, n_groups, per_group):
    q = (jnp.dot(x.astype(BF16), wq_ref[...], preferred_element_type=F32) * scale).astype(BF16)
    dh = q.shape[1] // n_heads
    outs = []
    for hh in range(n_heads):
        cols = slice(hh * dh, (hh + 1) * dh)
        s = lax.dot_general(q[:, cols], k_ref[:, cols], (((1,), (1,)), ((), ())), preferred_element_type=F32)
        e = jnp.exp(s - jnp.max(s, axis=-1, keepdims=True))
        pr = e / jnp.sum(e, axis=-1, keepdims=True)
        outs.append(jnp.dot(pr.astype(BF16), v_ref[:, cols], preferred_element_type=F32))
    o = jnp.concatenate(outs, axis=-1).astype(BF16)
    y = _layer_norm(alpha * x + jnp.dot(o, wo_ref[...], preferred_element_type=F32), g_ref[...], b_ref[...])
    o_ref[...] = y

    logits = jnp.dot(y.astype(BF16), wr_ref[...], preferred_element_type=F32) + br_ref[...]
    lane = lax.broadcasted_iota(jnp.int32, logits.shape, 1)
    far = jnp.int32(LANES)
    is_g = lane < n_groups
    gl = jnp.where(is_g, logits, -jnp.inf)
    gmax = jnp.max(gl, axis=-1, keepdims=True)
    grp = jnp.min(jnp.where(is_g & (gl == gmax), lane, far), axis=-1, keepdims=True)
    g_gate = 1.0 / jnp.sum(jnp.where(is_g, jnp.exp(gl - gmax), 0.0), axis=-1, keepdims=True)
    lo = n_groups + per_group * grp
    is_e = (lane >= lo) & (lane < lo + per_group)
    el = jnp.where(is_e, logits, -jnp.inf)
    v1 = jnp.max(el, axis=-1, keepdims=True)
    i1 = jnp.min(jnp.where(is_e & (el == v1), lane, far), axis=-1, keepdims=True)
    is_e2 = is_e & (lane != i1)
    el2 = jnp.where(is_e2, logits, -jnp.inf)
    v2 = jnp.max(el2, axis=-1, keepdims=True)
    i2 = jnp.min(jnp.where(is_e2 & (el2 == v2), lane, far), axis=-1, keepdims=True)
    e2 = jnp.exp(v2 - v1)
    w1 = g_gate / (1.0 + e2)
    w2 = g_gate * e2 / (1.0 + e2)
    r = jnp.where(lane == 0, (i1 - n_groups).astype(F32),
                  jnp.where(lane == 1, (i2 - n_groups).astype(F32),
                            jnp.where(lane == 2, w1, jnp.where(lane == 3, w2, 0.0))))
    r_ref[...] = r


def _mix_cross_router(parts, w_out_bf16, x, g1, b1, kv_bf16, wq_bf16, wo_bf16, g2, b2, wr_bf16, br,
                      *, seq, mem_len, alpha, per_group):
    t, d = x.shape
    cw = wq_bf16.shape[1]
    tm = min(TM_PROJ, seq)
    assert seq % tm == 0
    tps = seq // tm
    dh = cw // CROSS_HEADS
    n_exp = wr_bf16.shape[1]
    const = lambda shape: pl.BlockSpec(shape, lambda i: (0, 0), pipeline_mode=pl.Buffered(1))
    row = lambda v: v.reshape(1, d).astype(F32)
    in_specs, ws, off = [], [], 0
    for p in parts:
        kp = p.shape[1]
        in_specs.append(pl.BlockSpec((tm, kp), lambda i: (i, 0)))
        ws.append(w_out_bf16[off:off + kp])
        off += kp
    assert off == w_out_bf16.shape[0]
    in_specs += [const(w.shape) for w in ws]
    in_specs += [
        pl.BlockSpec((tm, d), lambda i: (i, 0)), const((1, d)), const((1, d)),
        const((d, cw)),
        pl.BlockSpec((mem_len, cw), lambda i: (i // tps, 0)),
        pl.BlockSpec((mem_len, cw), lambda i: (i // tps, 1)),
        const((cw, d)), const((1, d)), const((1, d)), const((d, n_exp)), const((1, n_exp)),
    ]
    return pl.pallas_call(
        functools.partial(_mix_cross_router_kernel, n_parts=len(parts), alpha=alpha, scale=dh ** -0.5,
                          n_heads=CROSS_HEADS, n_groups=N_GROUPS, per_group=per_group),
        grid=(t // tm,),
        in_specs=in_specs,
        out_specs=[pl.BlockSpec((tm, d), lambda i: (i, 0)), pl.BlockSpec((tm, LANES), lambda i: (i, 0))],
        out_shape=[jax.ShapeDtypeStruct((t, d), F32), jax.ShapeDtypeStruct((t, LANES), F32)],
        compiler_params=_cparams(1),
        name="mix_cross_router",
    )(*parts, *ws, x, row(g1), row(b1), wq_bf16, kv_bf16, kv_bf16, wo_bf16, row(g2), row(b2), wr_bf16, br)


def _moe_kernel(be_ref, nu_ref, idx0_ref, idx_ref, w1_ref, w3_ref, w2_ref,
                x_hbm, o_hbm, xb0, xb1, yb0, yb1, wb1, wb3, wb2, gsem, ssem):
    i = pl.program_id(0)
    n_used = nu_ref[0]
    tb = xb0.shape[0]
    xbufs, ybufs = (xb0, xb1), (yb0, yb1)

    def gather_row(tok, r, s):
        return pltpu.make_async_copy(x_hbm.at[pl.ds(tok, 1), :], xbufs[s].at[pl.ds(r, 1), :], gsem.at[s])

    def scatter_row(dst, r, s):
        return pltpu.make_async_copy(ybufs[s].at[pl.ds(r, 1), :], o_hbm.at[pl.ds(dst, 1), :], ssem.at[s])

    def wait_gather(s):
        pltpu.make_async_copy(x_hbm.at[pl.ds(0, tb), :], xbufs[s], gsem.at[s]).wait()

    def wait_scatter(s):
        pltpu.make_async_copy(ybufs[s], o_hbm.at[pl.ds(0, tb), :], ssem.at[s]).wait()

    def step(slot):
        other = 1 - slot

        @pl.when(i == 0)
        def _():
            def body(r, c):
                gather_row(idx0_ref[0, 0, r], r, 0).start()
                return c
            lax.fori_loop(0, tb, body, 0, unroll=8)
            yb1[...] = jnp.zeros(yb1.shape, F32)

        first_of_expert = jnp.logical_or(i == 0, be_ref[i] != be_ref[jnp.maximum(i - 1, 0)])

        @pl.when(first_of_expert)
        def _():
            wb1[...] = w1_ref[0, 0].astype(BF16)
            wb3[...] = w3_ref[0, 0].astype(BF16)
            wb2[...] = w2_ref[0, 0].astype(BF16)

        wait_gather(slot)
        for r in range(tb):
            gather_row(idx_ref[0, 0, r], r, other).start(priority=r % 2)
        for r in range(tb):
            scatter_row(idx_ref[0, 0, tb + r], r, other).start(priority=r % 2)
        xb = xbufs[slot][...].astype(BF16)
        h1 = jnp.dot(xb, wb1[...], preferred_element_type=F32)
        h3 = jnp.dot(xb, wb3[...], preferred_element_type=F32)
        hb = (h1 * jax.nn.sigmoid(h1) * h3).astype(BF16)
        ybufs[slot][...] = jnp.dot(hb, wb2[...], preferred_element_type=F32)
        wait_scatter(other)

    def drain(slot):
        other = 1 - slot
        wait_gather(slot)

        def body(r, c):
            scatter_row(idx_ref[0, 0, tb + r], r, other).start()
            return c
        lax.fori_loop(0, tb, body, 0, unroll=8)
        wait_scatter(other)

    for parity in range(2):
        @pl.when(jnp.logical_and(i < n_used, (i & 1) == parity))
        def _():
            step(parity)

        @pl.when(jnp.logical_and(i == n_used, (i & 1) == parity))
        def _():
            drain(parity)


def _moe_experts(x, blk_e, n_used, tok, dst, w1, w3, w2, *, layer, n_out_rows):
    t, d = x.shape
    nb = blk_e.shape[0]
    tb = tok.shape[0] // nb
    ff = w1.shape[-1]
    idx = jnp.concatenate([tok.reshape(nb, 1, tb), jnp.roll(dst.reshape(nb, 1, tb), 2, axis=0)], axis=-1)
    smem_blk = lambda f: pl.BlockSpec((1, 1, 2 * tb), f, memory_space=pltpu.SMEM)
    grid_spec = pltpu.PrefetchScalarGridSpec(
        num_scalar_prefetch=2,
        grid=(nb,),
        in_specs=[
            smem_blk(lambda i, be, nu: (0, 0, 0)),
            smem_blk(lambda i, be, nu: (jnp.minimum(i + 1, nb - 1), 0, 0)),
            pl.BlockSpec((1, 1, d, ff), lambda i, be, nu: (layer, be[i], 0, 0)),
            pl.BlockSpec((1, 1, d, ff), lambda i, be, nu: (layer, be[i], 0, 0)),
            pl.BlockSpec((1, 1, ff, d), lambda i, be, nu: (layer, be[i], 0, 0)),
            pl.BlockSpec(memory_space=pl.ANY),
        ],
        out_specs=pl.BlockSpec(memory_space=pl.ANY),
        scratch_shapes=[pltpu.VMEM((tb, d), F32), pltpu.VMEM((tb, d), F32),
                        pltpu.VMEM((tb, d), F32), pltpu.VMEM((tb, d), F32),
                        pltpu.VMEM((d, ff), BF16), pltpu.VMEM((d, ff), BF16), pltpu.VMEM((ff, d), BF16),
                        pltpu.SemaphoreType.DMA((2,)), pltpu.SemaphoreType.DMA((2,))],
    )
    return pl.pallas_call(
        _moe_kernel,
        grid_spec=grid_spec,
        out_shape=jax.ShapeDtypeStruct((n_out_rows, d), F32),
        compiler_params=_cparams(1),
        name="moe_experts",
    )(blk_e, n_used, idx, idx, w1, w3, w2, x)


def _moe_dispatch(route, *, n_experts, tb):
    t = route.shape[0]
    n_assign = TOP_K * t
    nb = n_assign // tb + n_experts + 1
    eid = route[:, :TOP_K].astype(jnp.int32).reshape(-1)
    assert n_experts * n_assign < 2 ** 31
    order = jnp.sort(eid * n_assign + jnp.arange(n_assign, dtype=jnp.int32)) % n_assign
    counts = jnp.sum((eid[:, None] == jnp.arange(n_experts, dtype=jnp.int32)[None, :]).astype(jnp.int32), axis=0)
    blocks = (counts + tb - 1) // tb
    bend = jnp.cumsum(blocks)
    start = jnp.cumsum(counts) - counts
    blk = jnp.arange(nb, dtype=jnp.int32)
    blk_e = jnp.minimum(jnp.sum((blk[:, None] >= bend[None, :]).astype(jnp.int32), axis=1), n_experts - 1)
    k_in_e = blk - (bend - blocks)[blk_e]
    n_valid = jnp.clip(counts[blk_e] - k_in_e * tb, 0, tb)
    r = jnp.arange(tb, dtype=jnp.int32)
    valid = r[None, :] < n_valid[:, None]
    src = jnp.where(valid, (start[blk_e] + k_in_e * tb)[:, None] + r[None, :], 0)
    a = order[src]
    a_tok = a // TOP_K
    tok = jnp.where(valid, a_tok, 0).reshape(-1)
    dst = jnp.where(valid, (a % TOP_K) * t + a_tok, TOP_K * t + r[None, :]).reshape(-1)
    return blk_e.astype(jnp.int32), bend[-1:].astype(jnp.int32), tok, dst


def _combine_ln_kernel(x_ref, y0_ref, y1_ref, r_ref, g_ref, b_ref, o_ref, *, alpha):
    w0 = r_ref[:, TOP_K:TOP_K + 1]
    w1 = r_ref[:, TOP_K + 1:TOP_K + 2]
    o_ref[...] = _layer_norm(alpha * x_ref[...] + w0 * y0_ref[...] + w1 * y1_ref[...], g_ref[...], b_ref[...])


def _combine_ln(x, y, route, g, b, *, alpha):
    t, d = x.shape
    tm = min(TM_PROJ, t)
    assert t % tm == 0
    nt = t // tm
    return pl.pallas_call(
        functools.partial(_combine_ln_kernel, alpha=alpha),
        grid=(nt,),
        in_specs=[pl.BlockSpec((tm, d), lambda i: (i, 0)),
                  pl.BlockSpec((tm, d), lambda i: (i, 0)),
                  pl.BlockSpec((tm, d), lambda i: (i + nt, 0)),
                  pl.BlockSpec((tm, LANES), lambda i: (i, 0)),
                  pl.BlockSpec((1, d), lambda i: (0, 0)), pl.BlockSpec((1, d), lambda i: (0, 0))],
        out_specs=pl.BlockSpec((tm, d), lambda i: (i, 0)),
        out_shape=jax.ShapeDtypeStruct((t, d), F32),
        compiler_params=_cparams(1),
        name="moe_combine_ln",
    )(x, y, y, route, g.reshape(1, d).astype(F32), b.reshape(1, d).astype(F32))


def _rope_head_perm(dh):
    r = dh // 8
    old = list(range(dh))
    rest = old[2 * r:]
    n_low = dh // 2 - r
    return jnp.asarray(old[:r] + rest[:n_low] + old[r:2 * r] + rest[n_low:], jnp.int32)


def _rope_tables(positions, dh):
    rope_dim = dh // 4
    half = rope_dim // 2
    assert dh == LANES
    inv_freq = ROPE_THETA ** (-jnp.arange(0, rope_dim, 2, dtype=F32) / rope_dim)
    ang = positions.astype(F32).reshape(-1, 1) * inv_freq[None, :]
    cos, sin = jnp.cos(ang), jnp.sin(ang)
    t = ang.shape[0]
    gap = dh // 2 - half
    c = jnp.concatenate([cos, jnp.ones((t, gap), F32), cos, jnp.ones((t, gap), F32)], axis=1)
    s = jnp.concatenate([-sin, jnp.zeros((t, gap), F32), sin, jnp.zeros((t, gap), F32)], axis=1)
    return c, s


def kernel(x, mem, positions, w_in, w_out, ln_mix_g, ln_mix_b, ln_mem_g, ln_mem_b, ln_ffn_g, ln_ffn_b, lam_q1, lam_k1, lam_q2, lam_k2, diff_subln_g, conv_w, conv_b, conv_ln_g, conv_ln_b, sgu_ln_g, sgu_ln_b, sgu_w, sgu_b, sc_w, mem_kv_w, xq_w, xo_w, rg_w, rg_b, re_w, re_b, e_w1, e_w3, e_w2):
    bsz, seq, d = x.shape
    t = bsz * seq
    depth = w_in.shape[0]
    mem_len = mem.shape[1]
    half = w_in.shape[2] // 5
    dh = lam_q1.shape[-1]
    n_experts = re_w.shape[-1]
    alpha = (2 * depth) ** 0.25
    assert half % TN_PROJ == 0 and diff_subln_g.shape[-1] == 2 * dh and half == 2 * DIFF_HEADS * dh

    xt = x.reshape(t, d)
    rope = _rope_tables(positions, dh)
    kv = _inproj(mem.reshape(bsz * mem_len, d), mem_kv_w.astype(BF16))

    for l in range(depth):
        j = l // 2
        w_l = w_in[l].astype(BF16)
        if l % 2 == 0:
            qk_cols = 2 * half
            col = jnp.arange(qk_cols, dtype=jnp.int32)
            col = (col // dh) * dh + _rope_head_perm(dh)[col % dh]
            w_l = jnp.concatenate([jnp.take(w_l[:, :qk_cols], col, axis=1), w_l[:, qk_cols:]], axis=1)
            h = _inproj(xt, w_l, rope, n_rope_tiles=2 * half // TN_PROJ, n_scale_tiles=half // TN_PROJ,
                        scale=dh ** -0.5 * math.log2(math.e))
            lam_init = 0.8 - 0.6 * math.exp(-0.3 * l)
            lam = (jnp.exp(jnp.sum(lam_q1[j].astype(F32) * lam_k1[j].astype(F32)))
                   - jnp.exp(jnp.sum(lam_q2[j].astype(F32) * lam_k2[j].astype(F32))) + lam_init)
            o = _diff_attention(h, lam, diff_subln_g[j], bsz=bsz, seq=seq, out_scale=1.0 - lam_init)
            c = _conformer(h, conv_w[j], conv_b[j], conv_ln_g[j], conv_ln_b[j], seq=seq, col_a=3, col_g=4)
            parts = [o, c]
        else:
            h = _inproj(xt, w_l)
            parts = [_sgu_shortconv(h, sgu_ln_g[j], sgu_ln_b[j], sgu_w[j], sgu_b[j], sc_w[j], seq=seq)]
        n_route = N_GROUPS + n_experts
        wr = jnp.concatenate([rg_w[l], re_w[l], jnp.zeros((d, LANES - n_route), F32)], axis=1).astype(BF16)
        br = jnp.concatenate([rg_b[l], re_b[l], jnp.zeros((LANES - n_route,), F32)]).reshape(1, LANES).astype(F32)
        xt, route = _mix_cross_router(parts, w_out[l].astype(BF16), xt, ln_mix_g[l], ln_mix_b[l], kv,
                                      xq_w[l].astype(BF16), xo_w[l].astype(BF16), ln_mem_g[l], ln_mem_b[l], wr, br,
                                      seq=seq, mem_len=mem_len, alpha=alpha, per_group=n_experts // N_GROUPS)

        blk_e, n_used, tok, dst = _moe_dispatch(route, n_experts=n_experts, tb=TB_MOE)
        y = _moe_experts(xt, blk_e, n_used, tok, dst, e_w1, e_w3, e_w2, layer=l, n_out_rows=TOP_K * t + TB_MOE)
        xt = _combine_ln(xt, y, route, ln_ffn_g[l], ln_ffn_b[l], alpha=alpha)
    return xt.reshape(bsz, seq, d)
```

```python
import functools
import math

import jax
import jax.numpy as jnp
from jax import lax
from jax.experimental import pallas as pl
from jax.experimental.pallas import tpu as pltpu

F32 = jnp.float32
BF16 = jnp.bfloat16

DIFF_HEADS = 4
ROPE_THETA = 500000.0
CONF_KERNEL = 31
SGU_CHUNK = 128
CROSS_HEADS = 4
N_GROUPS = 4
TOP_K = 2
LN_EPS = 1e-5

LANES = 128
SUBLANES = 8
NEG_BIG = -1e30
VMEM_LIMIT = 56 * 1024 * 1024

TM_PROJ = 512
TM_INPROJ = 1024
TN_PROJ = 1024
TQ_ATT = 1024
ATT_RC = 64
TR_MIX = 256
CONV_HALO = 32
CONV_RB = 32
SC_HALO = 16
TB_MOE = 256


def _cparams(n_axes):
    return pltpu.CompilerParams(dimension_semantics=("arbitrary",) * n_axes,
                                vmem_limit_bytes=VMEM_LIMIT)


def _layer_norm(y, g, b):
    mu = jnp.mean(y, axis=-1, keepdims=True)
    d = y - mu
    var = jnp.mean(d * d, axis=-1, keepdims=True)
    return d * lax.rsqrt(var + LN_EPS) * g + b


def _inproj_kernel(x_ref, w_ref, *rest, n_rope_tiles, n_scale_tiles, scale):
    if n_rope_tiles:
        c_ref, s_ref, o_ref, xb_ref = rest
    else:
        o_ref, xb_ref = rest
    j = pl.program_id(1)

    @pl.when(j == 0)
    def _():
        xb_ref[...] = x_ref[...].astype(BF16)

    acc = jnp.dot(xb_ref[...], w_ref[...], preferred_element_type=F32)
    if not n_rope_tiles:
        o_ref[...] = acc.astype(o_ref.dtype)
        return

    @pl.when(j >= n_rope_tiles)
    def _():
        o_ref[...] = acc.astype(o_ref.dtype)

    @pl.when(j < n_rope_tiles)
    def _():
        a = acc * jnp.where(j < n_scale_tiles, scale, 1.0).astype(F32)
        c, s = c_ref[...], s_ref[...]
        for g in range(a.shape[1] // LANES):
            ag = a[:, g * LANES:(g + 1) * LANES]
            og = ag * c + pltpu.roll(ag, LANES // 2, axis=1) * s
            o_ref[:, g * LANES:(g + 1) * LANES] = og.astype(o_ref.dtype)


def _inproj(x, w_bf16, rope=None, *, n_rope_tiles=0, n_scale_tiles=0, scale=1.0, out_dtype=BF16):
    m, k = x.shape
    n = w_bf16.shape[1]
    tm = min(TM_INPROJ, m)
    tn = min(TN_PROJ, n)
    assert m % tm == 0 and n % tn == 0
    in_specs = [pl.BlockSpec((tm, k), lambda i, j: (i, 0)),
                pl.BlockSpec((k, tn), lambda i, j: (0, j))]
    args = [x, w_bf16]
    if n_rope_tiles:
        in_specs += [pl.BlockSpec((tm, LANES), lambda i, j: (i, 0))] * 2
        args += list(rope)
    return pl.pallas_call(
        functools.partial(_inproj_kernel, n_rope_tiles=n_rope_tiles, n_scale_tiles=n_scale_tiles, scale=scale),
        grid=(m // tm, n // tn),
        in_specs=in_specs,
        out_specs=pl.BlockSpec((tm, tn), lambda i, j: (i, j)),
        out_shape=jax.ShapeDtypeStruct((m, n), out_dtype),
        scratch_shapes=[pltpu.VMEM((tm, k), BF16)],
        compiler_params=_cparams(2),
        name="inproj_rope" if n_rope_tiles else "inproj",
    )(*args)


def _diff_attn_kernel(lam_ref, q_ref, k_ref, v_ref, g_ref, o_ref, s_a, s_b, p_a, p_b, a_a, a_b,
                      m_ref, l_ref, acc_ref, *, dh, out_scale):
    qi = pl.program_id(2)
    tq = q_ref.shape[0]
    tk = s_a.shape[2]
    dv = v_ref.shape[1]
    ngrp = tk // LANES
    m_ref[...] = jnp.full(m_ref.shape, NEG_BIG, F32)
    l_ref[...] = jnp.zeros(l_ref.shape, F32)
    acc_ref[...] = jnp.zeros(acc_ref.shape, F32)

    def scores(j, s_ref):
        r0 = pl.multiple_of(j * tk, tk)
        for mi in range(2):
            cols = slice(mi * dh, (mi + 1) * dh)
            s_ref[mi] = lax.dot_general(q_ref[:, cols], k_ref[pl.ds(r0, tk), cols], (((1,), (1,)), ((), ())),
                                        preferred_element_type=F32)

    def softmax(s_ref, p_ref, a_ref, mask_shift=None):
        for mi in range(2):
            for rc in range(tq // ATT_RC):
                rows = slice(rc * ATT_RC, (rc + 1) * ATT_RC)
                groups = [s_ref[mi, rows, g * LANES:(g + 1) * LANES] for g in range(ngrp)]
                if mask_shift is not None:
                    row = lax.broadcasted_iota(jnp.int32, (ATT_RC, LANES), 0) + rc * ATT_RC
                    col = lax.broadcasted_iota(jnp.int32, (ATT_RC, LANES), 1) + mask_shift
                    groups = [jnp.where(col + g * LANES <= row, sg, NEG_BIG) for g, sg in enumerate(groups)]
                mx = functools.reduce(jnp.maximum, groups)
                m_prev = m_ref[mi, rows]
                m_new = jnp.maximum(m_prev, jnp.max(mx, axis=1, keepdims=True))
                alpha = jnp.exp2(m_prev - m_new)
                lsum = None
                for g, sg in enumerate(groups):
                    pg = jnp.exp2(sg - m_new)
                    lsum = pg if lsum is None else lsum + pg
                    p_ref[mi, rows, g * LANES:(g + 1) * LANES] = pg.astype(BF16)
                l_ref[mi, rows] = alpha * l_ref[mi, rows] + lsum
                m_ref[mi, rows] = m_new
                a_ref[mi, rows] = alpha

    def weighted_values(j, p_ref, a_ref):
        r0 = pl.multiple_of(j * tk, tk)
        v = v_ref[pl.ds(r0, tk), :]
        for mi in range(2):
            pv = jnp.dot(p_ref[mi], v, preferred_element_type=F32)
            alpha = a_ref[mi]
            acc_ref[mi] = jnp.concatenate(
                [acc_ref[mi, :, c * LANES:(c + 1) * LANES] * alpha for c in range(dv // LANES)], axis=1) + pv

    scores(0, s_a)

    def pair(u, carry):
        j = 2 * u
        scores(j + 1, s_b)
        softmax(s_a, p_a, a_a)
        weighted_values(j, p_a, a_a)
        scores(j + 2, s_a)
        softmax(s_b, p_b, a_b)
        weighted_values(j + 1, p_b, a_b)
        return carry

    lax.fori_loop(0, qi, pair, 0)
    scores(2 * qi + 1, s_b)
    softmax(s_a, p_a, a_a, mask_shift=0)
    weighted_values(2 * qi, p_a, a_a)
    softmax(s_b, p_b, a_b, mask_shift=tk)
    weighted_values(2 * qi + 1, p_b, a_b)

    lam = lam_ref[0, 0]
    l0 = jnp.sum(l_ref[0], axis=1, keepdims=True)
    l1 = jnp.sum(l_ref[1], axis=1, keepdims=True)
    o = acc_ref[0] / l0 - lam * (acc_ref[1] / l1)
    ms = jnp.mean(o * o, axis=-1, keepdims=True)
    o_ref[...] = (o * lax.rsqrt(ms + LN_EPS) * g_ref[...] * out_scale).astype(o_ref.dtype)


def _diff_attention(h, lam, subln_g, *, bsz, seq, out_scale):
    t = h.shape[0]
    dv = subln_g.shape[-1]
    dh = dv // 2
    tq = min(TQ_ATT, seq)
    tk = tq // 2
    nq = seq // tq
    assert seq % tq == 0 and tq % (2 * ATT_RC) == 0
    hd_n = DIFF_HEADS
    resident = lambda c0: pl.BlockSpec((seq, dv), lambda b, hd, qi: (b, c0 + hd), pipeline_mode=pl.Buffered(1))
    return pl.pallas_call(
        functools.partial(_diff_attn_kernel, dh=dh, out_scale=out_scale),
        grid=(bsz, hd_n, nq),
        in_specs=[
            pl.BlockSpec(memory_space=pltpu.SMEM),
            pl.BlockSpec((tq, dv), lambda b, hd, qi: (b * nq + qi, hd)),
            resident(hd_n), resident(2 * hd_n),
            pl.BlockSpec((1, dv), lambda b, hd, qi: (0, 0)),
        ],
        out_specs=pl.BlockSpec((tq, dv), lambda b, hd, qi: (b * nq + qi, hd)),
        out_shape=jax.ShapeDtypeStruct((t, hd_n * dv), BF16),
        scratch_shapes=[pltpu.VMEM((2, tq, tk), F32), pltpu.VMEM((2, tq, tk), F32),
                        pltpu.VMEM((2, tq, tk), BF16), pltpu.VMEM((2, tq, tk), BF16),
                        pltpu.VMEM((2, tq, LANES), F32), pltpu.VMEM((2, tq, LANES), F32),
                        pltpu.VMEM((2, tq, LANES), F32), pltpu.VMEM((2, tq, LANES), F32),
                        pltpu.VMEM((2, tq, dv), F32)],
        compiler_params=_cparams(3),
        name="diff_attn",
    )(lam.reshape(1, 1).astype(F32), h, h, h, subln_g.reshape(1, dv).astype(F32))


def _conformer_kernel(a_ref, g_ref, ah_ref, gh_ref, w_ref, cb_ref, lg_ref, lb_ref, o_ref, cext_ref, conv_ref,
                      *, tiles_per_seq):
    i = pl.program_id(0)
    tr, width = a_ref.shape
    halo = ah_ref.shape[0]
    ksz = w_ref.shape[0]
    first = (i % tiles_per_seq) == 0

    glu_h = ah_ref[...].astype(F32) * jax.nn.sigmoid(gh_ref[...].astype(F32))
    cext_ref[0:halo, :] = jnp.where(first, 0.0, glu_h)
    cext_ref[halo:halo + tr, :] = a_ref[...].astype(F32) * jax.nn.sigmoid(g_ref[...].astype(F32))
    cext_ref[halo + tr:halo + tr + SUBLANES, :] = jnp.zeros((SUBLANES, width), F32)

    base = halo - (ksz - 1)
    win = CONV_RB + halo + SUBLANES

    def chunk(rc, carry):
        r0 = pl.multiple_of(rc * CONV_RB, CONV_RB)
        for c in range(width // LANES):
            lanes = slice(c * LANES, (c + 1) * LANES)
            wnd = cext_ref[pl.ds(r0, win), lanes]
            acc = jnp.zeros((CONV_RB, LANES), F32)
            for b in range(SUBLANES):
                shifted = wnd if b == 0 else pltpu.roll(wnd, win - b, axis=0)
                for a in range((halo + SUBLANES) // SUBLANES):
                    j = SUBLANES * a + b - base
                    if 0 <= j < ksz:
                        acc = acc + w_ref[j:j + 1, lanes] * shifted[SUBLANES * a:SUBLANES * a + CONV_RB]
            conv_ref[pl.ds(r0, CONV_RB), lanes] = acc + cb_ref[:, lanes]
        return carry

    lax.fori_loop(0, tr // CONV_RB, chunk, 0)
    y = _layer_norm(conv_ref[...], lg_ref[...], lb_ref[...])
    o_ref[...] = (y * jax.nn.sigmoid(y)).astype(o_ref.dtype)


def _conformer(h, conv_w, conv_b, ln_g, ln_b, *, seq, col_a, col_g):
    t = h.shape[0]
    ksz, width = conv_w.shape
    tr = min(TR_MIX, seq)
    halo = CONV_HALO
    assert seq % tr == 0 and tr % halo == 0 and ksz - 1 <= halo and tr % CONV_RB == 0
    rpb = tr // halo
    row = lambda v: v.reshape(1, width).astype(F32)
    return pl.pallas_call(
        functools.partial(_conformer_kernel, tiles_per_seq=seq // tr),
        grid=(t // tr,),
        in_specs=[
            pl.BlockSpec((tr, width), lambda i: (i, col_a)),
            pl.BlockSpec((tr, width), lambda i: (i, col_g)),
            pl.BlockSpec((halo, width), lambda i: (jnp.maximum(i * rpb - 1, 0), col_a)),
            pl.BlockSpec((halo, width), lambda i: (jnp.maximum(i * rpb - 1, 0), col_g)),
            pl.BlockSpec((ksz, width), lambda i: (0, 0)),
            pl.BlockSpec((1, width), lambda i: (0, 0)),
            pl.BlockSpec((1, width), lambda i: (0, 0)),
            pl.BlockSpec((1, width), lambda i: (0, 0)),
        ],
        out_specs=pl.BlockSpec((tr, width), lambda i: (i, 0)),
        out_shape=jax.ShapeDtypeStruct((t, width), BF16),
        scratch_shapes=[pltpu.VMEM((halo + tr + SUBLANES, width), F32), pltpu.VMEM((tr, width), F32)],
        compiler_params=_cparams(1),
        name="conformer_conv",
    )(h, h, h, h, conv_w.astype(F32), row(conv_b), row(ln_g), row(ln_b))


def _gelu_exact(x):
    return 0.5 * x * (1.0 + lax.erf(x * math.sqrt(0.5)))


def _sgu_kernel(u_ref, v_ref, gb_ref, gc_ref, xi_ref, gch_ref, xih_ref, lg_ref, lb_ref, sw_ref, sb_ref, cw_ref,
                o_ref, pext_ref, *, tiles_per_seq):
    i = pl.program_id(0)
    tr, width = u_ref.shape
    halo = gch_ref.shape[0]
    n_grp, chunk, _ = sw_ref.shape
    gdim = width // n_grp
    first = (i % tiles_per_seq) == 0

    vg = _layer_norm(_gelu_exact(v_ref[...].astype(F32)), lg_ref[...], lb_ref[...]).astype(BF16)
    trow = lax.broadcasted_iota(jnp.int32, (chunk, chunk), 0)
    tcol = lax.broadcasted_iota(jnp.int32, (chunk, chunk), 1)
    for g in range(n_grp):
        wg = jnp.where(tcol <= trow, sw_ref[g], 0.0).astype(BF16)
        for n in range(tr // chunk):
            rows = slice(n * chunk, (n + 1) * chunk)
            cols = slice(g * gdim, (g + 1) * gdim)
            sv = jnp.dot(wg, vg[rows, cols], preferred_element_type=F32) + sb_ref[g]
            o_ref[rows, cols] = (_gelu_exact(u_ref[rows, cols].astype(F32)) * sv).astype(o_ref.dtype)

    ph = gch_ref[...].astype(F32) * xih_ref[...].astype(F32)
    pext_ref[0:halo, :] = jnp.where(first, 0.0, ph)
    pext_ref[halo:halo + tr, :] = gc_ref[...].astype(F32) * xi_ref[...].astype(F32)
    pe = pext_ref[...]
    n_ext = halo + tr
    ksz = cw_ref.shape[0]
    conv = cw_ref[ksz - 1:ksz, :] * pe[halo:]
    for back in range(1, ksz):
        conv = conv + cw_ref[ksz - 1 - back:ksz - back, :] * pltpu.roll(pe, back, axis=0)[halo:]
    del n_ext
    o_ref[:, width:2 * width] = (gb_ref[...].astype(F32) * conv).astype(o_ref.dtype)


def _sgu_shortconv(h, ln_g, ln_b, sgu_w, sgu_b, sc_w, *, seq):
    t = h.shape[0]
    width = ln_g.shape[-1]
    n_grp, chunk, _ = sgu_w.shape
    gdim = width // n_grp
    tr = min(TR_MIX, seq)
    halo = SC_HALO
    assert seq % tr == 0 and tr % chunk == 0 and tr % halo == 0 and sc_w.shape[0] - 1 <= halo
    rpb = tr // halo
    row = lambda v: v.reshape(1, width).astype(F32)
    bias = jnp.broadcast_to(sgu_b.astype(F32)[:, :, None], (n_grp, chunk, gdim))
    cur = lambda c: pl.BlockSpec((tr, width), lambda i: (i, c))
    prev = lambda c: pl.BlockSpec((halo, width), lambda i: (jnp.maximum(i * rpb - 1, 0), c))
    whole = lambda a: pl.BlockSpec(a.shape, lambda i: (0,) * a.ndim)
    small = [row(ln_g), row(ln_b), sgu_w.astype(F32), bias, sc_w.astype(F32)]
    return pl.pallas_call(
        functools.partial(_sgu_kernel, tiles_per_seq=seq // tr),
        grid=(t // tr,),
        in_specs=[cur(0), cur(1), cur(2), cur(3), cur(4), prev(3), prev(4)] + [whole(a) for a in small],
        out_specs=pl.BlockSpec((tr, 2 * width), lambda i: (i, 0)),
        out_shape=jax.ShapeDtypeStruct((t, 2 * width), BF16),
        scratch_shapes=[pltpu.VMEM((halo + tr, width), F32)],
        compiler_params=_cparams(1),
        name="sgu_shortconv",
    )(h, h, h, h, h, h, h, *small)


def _mix_cross_router_kernel(*refs, n_parts, alpha, scale, n_heads, n_groups, per_group):
    parts = refs[:n_parts]
    ws = refs[n_parts:2 * n_parts]
    (x_ref, g1_ref, b1_ref, wq_ref, k_ref, v_ref, wo_ref, g2_ref, b2_ref, wr_ref, br_ref,
     o_ref, r_ref) = refs[2 * n_parts:]
    half = x_ref.shape[0] // 2
    rows_a, rows_b = slice(0, half), slice(half, 2 * half)

    def out_projection(rows):
        acc = alpha * x_ref[rows, :]
        for p_ref, w_ref in zip(parts, ws):
            acc = acc + jnp.dot(p_ref[rows, :], w_ref[...], preferred_element_type=F32)
        return acc

    def norm_mix(acc):
        return _layer_norm(acc, g1_ref[...], b1_ref[...])

    def query(x1):
        return (jnp.dot(x1.astype(BF16), wq_ref[...], preferred_element_type=F32) * scale).astype(BF16)

    def attend(q):
        dh = q.shape[1] // n_heads
        outs = []
        for hh in range(n_heads):
            cols = slice(hh * dh, (hh + 1) * dh)
            s = lax.dot_general(q[:, cols], k_ref[:, cols], (((1,), (1,)), ((), ())), preferred_element_type=F32)
            e = jnp.exp(s - jnp.max(s, axis=-1, keepdims=True))
            pr = e / jnp.sum(e, axis=-1, keepdims=True)
            outs.append(jnp.dot(pr.astype(BF16), v_ref[:, cols], preferred_element_type=F32))
        return jnp.concatenate(outs, axis=-1).astype(BF16)

    def residual(x1, o):
        return alpha * x1 + jnp.dot(o, wo_ref[...], preferred_element_type=F32)

    def norm_mem(z, rows):
        y = _layer_norm(z, g2_ref[...], b2_ref[...])
        o_ref[rows, :] = y
        return y

    def route(y, rows):
        logits = jnp.dot(y.astype(BF16), wr_ref[...], preferred_element_type=F32) + br_ref[...]
        lane = lax.broadcasted_iota(jnp.int32, logits.shape, 1)
        far = jnp.int32(LANES)
        is_g = lane < n_groups
        gl = jnp.where(is_g, logits, -jnp.inf)
        gmax = jnp.max(gl, axis=-1, keepdims=True)
        grp = jnp.min(jnp.where(is_g & (gl == gmax), lane, far), axis=-1, keepdims=True)
        g_gate = 1.0 / jnp.sum(jnp.where(is_g, jnp.exp(gl - gmax), 0.0), axis=-1, keepdims=True)
        lo = n_groups + per_group * grp
        is_e = (lane >= lo) & (lane < lo + per_group)
        el = jnp.where(is_e, logits, -jnp.inf)
        v1 = jnp.max(el, axis=-1, keepdims=True)
        i1 = jnp.min(jnp.where(is_e & (el == v1), lane, far), axis=-1, keepdims=True)
        is_e2 = is_e & (lane != i1)
        el2 = jnp.where(is_e2, logits, -jnp.inf)
        v2 = jnp.max(el2, axis=-1, keepdims=True)
        i2 = jnp.min(jnp.where(is_e2 & (el2 == v2), lane, far), axis=-1, keepdims=True)
        e2 = jnp.exp(v2 - v1)
        w1 = g_gate / (1.0 + e2)
        w2 = g_gate * e2 / (1.0 + e2)
        r_ref[rows, :] = jnp.where(lane == 0, (i1 - n_groups).astype(F32),
                                   jnp.where(lane == 1, (i2 - n_groups).astype(F32),
                                             jnp.where(lane == 2, w1, jnp.where(lane == 3, w2, 0.0))))

    acc_a = out_projection(rows_a)
    acc_b = out_projection(rows_b)
    x1_a = norm_mix(acc_a)
    q_a = query(x1_a)
    x1_b = norm_mix(acc_b)
    o_a = attend(q_a)
    q_b = query(x1_b)
    z_a = residual(x1_a, o_a)
    o_b = attend(q_b)
    y_a = norm_mem(z_a, rows_a)
    z_b = residual(x1_b, o_b)
    route(y_a, rows_a)
    y_b = norm_mem(z_b, rows_b)
    route(y_b, rows_b)


def _mix_cross_router(parts, w_out_bf16, x, g1, b1, kv_bf16, wq_bf16, wo_bf16, g2, b2, wr_bf16, br,
                      *, seq, mem_len, alpha, per_group):
    t, d = x.shape
    cw = wq_bf16.shape[1]
    tm = min(TM_PROJ, seq)
    assert seq % tm == 0
    tps = seq // tm
    dh = cw // CROSS_HEADS
    n_exp = wr_bf16.shape[1]
    const = lambda shape: pl.BlockSpec(shape, lambda i: (0, 0), pipeline_mode=pl.Buffered(1))
    row = lambda v: v.reshape(1, d).astype(F32)
    in_specs, ws, off = [], [], 0
    for p in parts:
        kp = p.shape[1]
        in_specs.append(pl.BlockSpec((tm, kp), lambda i: (i, 0)))
        ws.append(w_out_bf16[off:off + kp])
        off += kp
    assert off == w_out_bf16.shape[0]
    in_specs += [const(w.shape) for w in ws]
    in_specs += [
        pl.BlockSpec((tm, d), lambda i: (i, 0)), const((1, d)), const((1, d)),
        const((d, cw)),
        pl.BlockSpec((mem_len, cw), lambda i: (i // tps, 0)),
        pl.BlockSpec((mem_len, cw), lambda i: (i // tps, 1)),
        const((cw, d)), const((1, d)), const((1, d)), const((d, n_exp)), const((1, n_exp)),
    ]
    return pl.pallas_call(
        functools.partial(_mix_cross_router_kernel, n_parts=len(parts), alpha=alpha, scale=dh ** -0.5,
                          n_heads=CROSS_HEADS, n_groups=N_GROUPS, per_group=per_group),
        grid=(t // tm,),
        in_specs=in_specs,
        out_specs=[pl.BlockSpec((tm, d), lambda i: (i, 0)), pl.BlockSpec((tm, LANES), lambda i: (i, 0))],
        out_shape=[jax.ShapeDtypeStruct((t, d), F32), jax.ShapeDtypeStruct((t, LANES), F32)],
        compiler_params=_cparams(1),
        name="mix_cross_router",
    )(*parts, *ws, x, row(g1), row(b1), wq_bf16, kv_bf16, kv_bf16, wo_bf16, row(g2), row(b2), wr_bf16, br)


def _moe_kernel(be_ref, nu_ref, idx0_ref, idx_ref, w1_ref, w3_ref, w2_ref,
                x_hbm, o_hbm, xb0, xb1, yb0, yb1, wb1, wb3, wb2, gsem, ssem):
    i = pl.program_id(0)
    n_used = nu_ref[0]
    tb = xb0.shape[0]
    xbufs, ybufs = (xb0, xb1), (yb0, yb1)

    def gather_row(tok, r, s):
        return pltpu.make_async_copy(x_hbm.at[pl.ds(tok, 1), :], xbufs[s].at[pl.ds(r, 1), :], gsem.at[s])

    def scatter_row(dst, r, s):
        return pltpu.make_async_copy(ybufs[s].at[pl.ds(r, 1), :], o_hbm.at[pl.ds(dst, 1), :], ssem.at[s])

    def wait_gather(s):
        pltpu.make_async_copy(x_hbm.at[pl.ds(0, tb), :], xbufs[s], gsem.at[s]).wait()

    def wait_scatter(s):
        pltpu.make_async_copy(ybufs[s], o_hbm.at[pl.ds(0, tb), :], ssem.at[s]).wait()

    def step(slot):
        other = 1 - slot

        @pl.when(i == 0)
        def _():
            def body(r, c):
                gather_row(idx0_ref[0, 0, r], r, 0).start()
                return c
            lax.fori_loop(0, tb, body, 0, unroll=8)
            yb1[...] = jnp.zeros(yb1.shape, F32)

        first_of_expert = jnp.logical_or(i == 0, be_ref[i] != be_ref[jnp.maximum(i - 1, 0)])

        @pl.when(first_of_expert)
        def _():
            wb1[...] = w1_ref[0, 0].astype(BF16)
            wb3[...] = w3_ref[0, 0].astype(BF16)
            wb2[...] = w2_ref[0, 0].astype(BF16)

        wait_gather(slot)
        for r in range(tb):
            gather_row(idx_ref[0, 0, r], r, other).start(priority=r % 2)
        for r in range(tb):
            scatter_row(idx_ref[0, 0, tb + r], r, other).start(priority=r % 2)
        xb = xbufs[slot][...].astype(BF16)
        h1 = jnp.dot(xb, wb1[...], preferred_element_type=F32)
        h3 = jnp.dot(xb, wb3[...], preferred_element_type=F32)
        hb = (h1 * jax.nn.sigmoid(h1) * h3).astype(BF16)
        ybufs[slot][...] = jnp.dot(hb, wb2[...], preferred_element_type=F32)
        wait_scatter(other)

    def drain(slot):
        other = 1 - slot
        wait_gather(slot)

        def body(r, c):
            scatter_row(idx_ref[0, 0, tb + r], r, other).start()
            return c
        lax.fori_loop(0, tb, body, 0, unroll=8)
        wait_scatter(other)

    for parity in range(2):
        @pl.when(jnp.logical_and(i < n_used, (i & 1) == parity))
        def _():
            step(parity)

        @pl.when(jnp.logical_and(i == n_used, (i & 1) == parity))
        def _():
            drain(parity)


def _moe_experts(x, blk_e, n_used, tok, dst, w1, w3, w2, *, layer, n_out_rows):
    t, d = x.shape
    nb = blk_e.shape[0]
    tb = tok.shape[0] // nb
    ff = w1.shape[-1]
    idx = jnp.concatenate([tok.reshape(nb, 1, tb), jnp.roll(dst.reshape(nb, 1, tb), 2, axis=0)], axis=-1)
    smem_blk = lambda f: pl.BlockSpec((1, 1, 2 * tb), f, memory_space=pltpu.SMEM)
    grid_spec = pltpu.PrefetchScalarGridSpec(
        num_scalar_prefetch=2,
        grid=(nb,),
        in_specs=[
            smem_blk(lambda i, be, nu: (0, 0, 0)),
            smem_blk(lambda i, be, nu: (jnp.minimum(i + 1, nb - 1), 0, 0)),
            pl.BlockSpec((1, 1, d, ff), lambda i, be, nu: (layer, be[i], 0, 0)),
            pl.BlockSpec((1, 1, d, ff), lambda i, be, nu: (layer, be[i], 0, 0)),
            pl.BlockSpec((1, 1, ff, d), lambda i, be, nu: (layer, be[i], 0, 0)),
            pl.BlockSpec(memory_space=pl.ANY),
        ],
        out_specs=pl.BlockSpec(memory_space=pl.ANY),
        scratch_shapes=[pltpu.VMEM((tb, d), F32), pltpu.VMEM((tb, d), F32),
                        pltpu.VMEM((tb, d), F32), pltpu.VMEM((tb, d), F32),
                        pltpu.VMEM((d, ff), BF16), pltpu.VMEM((d, ff), BF16), pltpu.VMEM((ff, d), BF16),
                        pltpu.SemaphoreType.DMA((2,)), pltpu.SemaphoreType.DMA((2,))],
    )
    return pl.pallas_call(
        _moe_kernel,
        grid_spec=grid_spec,
        out_shape=jax.ShapeDtypeStruct((n_out_rows, d), F32),
        compiler_params=_cparams(1),
        name="moe_experts",
    )(blk_e, n_used, idx, idx, w1, w3, w2, x)


def _moe_dispatch(route, *, n_experts, tb):
    t = route.shape[0]
    n_assign = TOP_K * t
    nb = n_assign // tb + n_experts + 1
    eid = route[:, :TOP_K].astype(jnp.int32).reshape(-1)
    assert n_experts * n_assign < 2 ** 31
    order = jnp.sort(eid * n_assign + jnp.arange(n_assign, dtype=jnp.int32)) % n_assign
    counts = jnp.sum((eid[:, None] == jnp.arange(n_experts, dtype=jnp.int32)[None, :]).astype(jnp.int32), axis=0)
    blocks = (counts + tb - 1) // tb
    bend = jnp.cumsum(blocks)
    start = jnp.cumsum(counts) - counts
    blk = jnp.arange(nb, dtype=jnp.int32)
    blk_e = jnp.minimum(jnp.sum((blk[:, None] >= bend[None, :]).astype(jnp.int32), axis=1), n_experts - 1)
    k_in_e = blk - (bend - blocks)[blk_e]
    n_valid = jnp.clip(counts[blk_e] - k_in_e * tb, 0, tb)
    r = jnp.arange(tb, dtype=jnp.int32)
    valid = r[None, :] < n_valid[:, None]
    src = jnp.where(valid, (start[blk_e] + k_in_e * tb)[:, None] + r[None, :], 0)
    a = order[src]
    a_tok = a // TOP_K
    tok = jnp.where(valid, a_tok, 0).reshape(-1)
    dst = jnp.where(valid, (a % TOP_K) * t + a_tok, TOP_K * t + r[None, :]).reshape(-1)
    return blk_e.astype(jnp.int32), bend[-1:].astype(jnp.int32), tok, dst


def _combine_ln_kernel(x_ref, y0_ref, y1_ref, r_ref, g_ref, b_ref, o_ref, *, alpha):
    w0 = r_ref[:, TOP_K:TOP_K + 1]
    w1 = r_ref[:, TOP_K + 1:TOP_K + 2]
    o_ref[...] = _layer_norm(alpha * x_ref[...] + w0 * y0_ref[...] + w1 * y1_ref[...], g_ref[...], b_ref[...])


def _combine_ln(x, y, route, g, b, *, alpha):
    t, d = x.shape
    tm = min(TM_PROJ, t)
    assert t % tm == 0
    nt = t // tm
    return pl.pallas_call(
        functools.partial(_combine_ln_kernel, alpha=alpha),
        grid=(nt,),
        in_specs=[pl.BlockSpec((tm, d), lambda i: (i, 0)),
                  pl.BlockSpec((tm, d), lambda i: (i, 0)),
                  pl.BlockSpec((tm, d), lambda i: (i + nt, 0)),
                  pl.BlockSpec((tm, LANES), lambda i: (i, 0)),
                  pl.BlockSpec((1, d), lambda i: (0, 0)), pl.BlockSpec((1, d), lambda i: (0, 0))],
        out_specs=pl.BlockSpec((tm, d), lambda i: (i, 0)),
        out_shape=jax.ShapeDtypeStruct((t, d), F32),
        compiler_params=_cparams(1),
        name="moe_combine_ln",
    )(x, y, y, route, g.reshape(1, d).astype(F32), b.reshape(1, d).astype(F32))


def _rope_head_perm(dh):
    r = dh // 8
    old = list(range(dh))
    rest = old[2 * r:]
    n_low = dh // 2 - r
    return jnp.asarray(old[:r] + rest[:n_low] + old[r:2 * r] + rest[n_low:], jnp.int32)


def _rope_tables(positions, dh):
    rope_dim = dh // 4
    half = rope_dim // 2
    assert dh == LANES
    inv_freq = ROPE_THETA ** (-jnp.arange(0, rope_dim, 2, dtype=F32) / rope_dim)
    ang = positions.astype(F32).reshape(-1, 1) * inv_freq[None, :]
    cos, sin = jnp.cos(ang), jnp.sin(ang)
    t = ang.shape[0]
    gap = dh // 2 - half
    c = jnp.concatenate([cos, jnp.ones((t, gap), F32), cos, jnp.ones((t, gap), F32)], axis=1)
    s = jnp.concatenate([-sin, jnp.zeros((t, gap), F32), sin, jnp.zeros((t, gap), F32)], axis=1)
    return c, s


def kernel(x, mem, positions, w_in, w_out, ln_mix_g, ln_mix_b, ln_mem_g, ln_mem_b, ln_ffn_g, ln_ffn_b, lam_q1, lam_k1, lam_q2, lam_k2, diff_subln_g, conv_w, conv_b, conv_ln_g, conv_ln_b, sgu_ln_g, sgu_ln_b, sgu_w, sgu_b, sc_w, mem_kv_w, xq_w, xo_w, rg_w, rg_b, re_w, re_b, e_w1, e_w3, e_w2):
    bsz, seq, d = x.shape
    t = bsz * seq
    depth = w_in.shape[0]
    mem_len = mem.shape[1]
    half = w_in.shape[2] // 5
    dh = lam_q1.shape[-1]
    n_experts = re_w.shape[-1]
    alpha = (2 * depth) ** 0.25
    assert half % TN_PROJ == 0 and diff_subln_g.shape[-1] == 2 * dh and half == 2 * DIFF_HEADS * dh

    xt = x.reshape(t, d)
    rope = _rope_tables(positions, dh)
    kv = _inproj(mem.reshape(bsz * mem_len, d), mem_kv_w.astype(BF16))

    for l in range(depth):
        j = l // 2
        w_l = w_in[l].astype(BF16)
        if l % 2 == 0:
            qk_cols = 2 * half
            col = jnp.arange(qk_cols, dtype=jnp.int32)
            col = (col // dh) * dh + _rope_head_perm(dh)[col % dh]
            w_l = jnp.concatenate([jnp.take(w_l[:, :qk_cols], col, axis=1), w_l[:, qk_cols:]], axis=1)
            h = _inproj(xt, w_l, rope, n_rope_tiles=2 * half // TN_PROJ, n_scale_tiles=half // TN_PROJ,
                        scale=dh ** -0.5 * math.log2(math.e))
            lam_init = 0.8 - 0.6 * math.exp(-0.3 * l)
            lam = (jnp.exp(jnp.sum(lam_q1[j].astype(F32) * lam_k1[j].astype(F32)))
                   - jnp.exp(jnp.sum(lam_q2[j].astype(F32) * lam_k2[j].astype(F32))) + lam_init)
            o = _diff_attention(h, lam, diff_subln_g[j], bsz=bsz, seq=seq, out_scale=1.0 - lam_init)
            c = _conformer(h, conv_w[j], conv_b[j], conv_ln_g[j], conv_ln_b[j], seq=seq, col_a=3, col_g=4)
            parts = [o, c]
        else:
            h = _inproj(xt, w_l)
            parts = [_sgu_shortconv(h, sgu_ln_g[j], sgu_ln_b[j], sgu_w[j], sgu_b[j], sc_w[j], seq=seq)]
        n_route = N_GROUPS + n_experts
        wr = jnp.concatenate([rg_w[l], re_w[l], jnp.zeros((d, LANES - n_route), F32)], axis=1).astype(BF16)
        br = jnp.concatenate([rg_b[l], re_b[l], jnp.zeros((LANES - n_route,), F32)]).reshape(1, LANES).astype(F32)
        xt, route = _mix_cross_router(parts, w_out[l].astype(BF16), xt, ln_mix_g[l], ln_mix_b[l], kv,
                                      xq_w[l].astype(BF16), xo_w[l].astype(BF16), ln_mem_g[l], ln_mem_b[l], wr, br,
                                      seq=seq, mem_len=mem_len, alpha=alpha, per_group=n_experts // N_GROUPS)

        blk_e, n_used, tok, dst = _moe_dispatch(route, n_experts=n_experts, tb=TB_MOE)
        y = _moe_experts(xt, blk_e, n_used, tok, dst, e_w1, e_w3, e_w2, layer=l, n_out_rows=TOP_K * t + TB_MOE)
        xt = _combine_ln(xt, y, route, ln_ffn_g[l], ln_ffn_b[l], alpha=alpha)
    return xt.reshape(bsz, seq, d)
```

```python
import functools
import math

import jax
import jax.numpy as jnp
from jax import lax
from jax.experimental import pallas as pl
from jax.experimental.pallas import tpu as pltpu

F32 = jnp.float32
BF16 = jnp.bfloat16

DIFF_HEADS = 4
ROPE_THETA = 500000.0
CROSS_HEADS = 4
N_GROUPS = 4
TOP_K = 2
LN_EPS = 1e-5

LANES = 128
SUBLANES = 8
NEG_BIG = -1e30
VMEM_LIMIT = 56 * 1024 * 1024

TM_PROJ = 512
TM_INPROJ = 1024
TN_PROJ = 1024
TQ_ATT = 1024
ATT_RC = 64
TR_MIX = 256
CONV_HALO = 32
CONV_RB = 32
SC_HALO = 16
TB_MOE = 256


def _cparams(n_axes):
    return pltpu.CompilerParams(dimension_semantics=("arbitrary",) * n_axes,
                                vmem_limit_bytes=VMEM_LIMIT)


def _layer_norm(y, g, b):
    mu = jnp.mean(y, axis=-1, keepdims=True)
    d = y - mu
    var = jnp.mean(d * d, axis=-1, keepdims=True)
    return d * lax.rsqrt(var + LN_EPS) * g + b


def _inproj_kernel(x_ref, w_ref, *rest, n_rope_tiles, n_scale_tiles, scale):
    if n_rope_tiles:
        c_ref, s_ref, o_ref, xb_ref = rest
    else:
        o_ref, xb_ref = rest
    j = pl.program_id(1)

    @pl.when(j == 0)
    def _():
        xb_ref[...] = x_ref[...].astype(BF16)

    def plain():
        o_ref[...] = jnp.dot(xb_ref[...], w_ref[...], preferred_element_type=F32).astype(o_ref.dtype)

    if not n_rope_tiles:
        plain()
        return

    pl.when(j >= n_rope_tiles)(plain)

    @pl.when(j < n_rope_tiles)
    def _():
        q_scale = jnp.where(j < n_scale_tiles, scale, 1.0).astype(F32)
        tm = o_ref.shape[0]
        quarters = [slice(i * (tm // 4), (i + 1) * (tm // 4)) for i in range(4)]

        def product(rows):
            return jnp.dot(xb_ref[rows, :], w_ref[...], preferred_element_type=F32)

        def rotary(rows, acc):
            a = acc * q_scale
            c, s = c_ref[rows, :], s_ref[rows, :]
            for g in range(a.shape[1] // LANES):
                ag = a[:, g * LANES:(g + 1) * LANES]
                og = ag * c + pltpu.roll(ag, LANES // 2, axis=1) * s
                o_ref[rows, g * LANES:(g + 1) * LANES] = og.astype(o_ref.dtype)

        accs = [product(quarters[0])]
        for i in range(4):
            if i + 1 < 4:
                accs.append(product(quarters[i + 1]))
            rotary(quarters[i], accs[i])


def _inproj(x, w_bf16, rope=None, *, n_rope_tiles=0, n_scale_tiles=0, scale=1.0, out_dtype=BF16):
    m, k = x.shape
    n = w_bf16.shape[1]
    tm = min(TM_INPROJ, m)
    tn = min(TN_PROJ, n)
    assert m % tm == 0 and n % tn == 0
    in_specs = [pl.BlockSpec((tm, k), lambda i, j: (i, 0)),
                pl.BlockSpec((k, tn), lambda i, j: (0, j))]
    args = [x, w_bf16]
    if n_rope_tiles:
        in_specs += [pl.BlockSpec((tm, LANES), lambda i, j: (i, 0))] * 2
        args += list(rope)
    return pl.pallas_call(
        functools.partial(_inproj_kernel, n_rope_tiles=n_rope_tiles, n_scale_tiles=n_scale_tiles, scale=scale),
        grid=(m // tm, n // tn),
        in_specs=in_specs,
        out_specs=pl.BlockSpec((tm, tn), lambda i, j: (i, j)),
        out_shape=jax.ShapeDtypeStruct((m, n), out_dtype),
        scratch_shapes=[pltpu.VMEM((tm, k), BF16)],
        compiler_params=_cparams(2),
        name="inproj_rope" if n_rope_tiles else "inproj",
    )(*args)


def _diff_attn_kernel(lam_ref, q_ref, k_ref, v_ref, g_ref, o_ref, s_a, s_b, p_a, p_b, a_a, a_b,
                      m_ref, l_ref, acc_ref, *, dh, out_scale):
    qi = pl.program_id(2)
    tq = q_ref.shape[0]
    tk = s_a.shape[2]
    dv = v_ref.shape[1]
    ngrp = tk // LANES
    m_ref[...] = jnp.full(m_ref.shape, NEG_BIG, F32)
    l_ref[...] = jnp.zeros(l_ref.shape, F32)
    acc_ref[...] = jnp.zeros(acc_ref.shape, F32)

    def scores(j, s_ref):
        r0 = pl.multiple_of(j * tk, tk)
        for mi in range(2):
            cols = slice(mi * dh, (mi + 1) * dh)
            s_ref[mi] = lax.dot_general(q_ref[:, cols], k_ref[pl.ds(r0, tk), cols], (((1,), (1,)), ((), ())),
                                        preferred_element_type=F32)

    def softmax(s_ref, p_ref, a_ref, mask_shift=None):
        for mi in range(2):
            for rc in range(tq // ATT_RC):
                rows = slice(rc * ATT_RC, (rc + 1) * ATT_RC)
                groups = [s_ref[mi, rows, g * LANES:(g + 1) * LANES] for g in range(ngrp)]
                if mask_shift is not None:
                    row = lax.broadcasted_iota(jnp.int32, (ATT_RC, LANES), 0) + rc * ATT_RC
                    col = lax.broadcasted_iota(jnp.int32, (ATT_RC, LANES), 1) + mask_shift
                    groups = [jnp.where(col + g * LANES <= row, sg, NEG_BIG) for g, sg in enumerate(groups)]
                mx = functools.reduce(jnp.maximum, groups)
                m_prev = m_ref[mi, rows]
                m_new = jnp.maximum(m_prev, jnp.max(mx, axis=1, keepdims=True))
                alpha = jnp.exp2(m_prev - m_new)
                lsum = None
                for g, sg in enumerate(groups):
                    pg = jnp.exp2(sg - m_new)
                    lsum = pg if lsum is None else lsum + pg
                    p_ref[mi, rows, g * LANES:(g + 1) * LANES] = pg.astype(BF16)
                l_ref[mi, rows] = alpha * l_ref[mi, rows] + lsum
                m_ref[mi, rows] = m_new
                a_ref[mi, rows] = alpha

    def weighted_values(j, p_ref, a_ref):
        r0 = pl.multiple_of(j * tk, tk)
        v = v_ref[pl.ds(r0, tk), :]
        for mi in range(2):
            pv = jnp.dot(p_ref[mi], v, preferred_element_type=F32)
            alpha = a_ref[mi]
            acc_ref[mi] = jnp.concatenate(
                [acc_ref[mi, :, c * LANES:(c + 1) * LANES] * alpha for c in range(dv // LANES)], axis=1) + pv

    scores(0, s_a)

    def pair(u, carry):
        j = 2 * u
        scores(j + 1, s_b)
        softmax(s_a, p_a, a_a)
        weighted_values(j, p_a, a_a)
        scores(j + 2, s_a)
        softmax(s_b, p_b, a_b)
        weighted_values(j + 1, p_b, a_b)
        return carry

    lax.fori_loop(0, qi, pair, 0)
    scores(2 * qi + 1, s_b)
    softmax(s_a, p_a, a_a, mask_shift=0)
    weighted_values(2 * qi, p_a, a_a)
    softmax(s_b, p_b, a_b, mask_shift=tk)
    weighted_values(2 * qi + 1, p_b, a_b)

    lam = lam_ref[0, 0]
    l0 = jnp.sum(l_ref[0], axis=1, keepdims=True)
    l1 = jnp.sum(l_ref[1], axis=1, keepdims=True)
    o = acc_ref[0] / l0 - lam * (acc_ref[1] / l1)
    ms = jnp.mean(o * o, axis=-1, keepdims=True)
    o_ref[...] = (o * lax.rsqrt(ms + LN_EPS) * g_ref[...] * out_scale).astype(o_ref.dtype)


def _diff_attention(h, lam, subln_g, *, bsz, seq, out_scale):
    t = h.shape[0]
    dv = subln_g.shape[-1]
    dh = dv // 2
    tq = min(TQ_ATT, seq)
    tk = tq // 2
    nq = seq // tq
    assert seq % tq == 0 and tq % (2 * ATT_RC) == 0
    hd_n = DIFF_HEADS
    resident = lambda c0: pl.BlockSpec((seq, dv), lambda b, hd, qi: (b, c0 + hd), pipeline_mode=pl.Buffered(1))
    return pl.pallas_call(
        functools.partial(_diff_attn_kernel, dh=dh, out_scale=out_scale),
        grid=(bsz, hd_n, nq),
        in_specs=[
            pl.BlockSpec(memory_space=pltpu.SMEM),
            pl.BlockSpec((tq, dv), lambda b, hd, qi: (b * nq + qi, hd)),
            resident(hd_n), resident(2 * hd_n),
            pl.BlockSpec((1, dv), lambda b, hd, qi: (0, 0)),
        ],
        out_specs=pl.BlockSpec((tq, dv), lambda b, hd, qi: (b * nq + qi, hd)),
        out_shape=jax.ShapeDtypeStruct((t, hd_n * dv), BF16),
        scratch_shapes=[pltpu.VMEM((2, tq, tk), F32), pltpu.VMEM((2, tq, tk), F32),
                        pltpu.VMEM((2, tq, tk), BF16), pltpu.VMEM((2, tq, tk), BF16),
                        pltpu.VMEM((2, tq, LANES), F32), pltpu.VMEM((2, tq, LANES), F32),
                        pltpu.VMEM((2, tq, LANES), F32), pltpu.VMEM((2, tq, LANES), F32),
                        pltpu.VMEM((2, tq, dv), F32)],
        compiler_params=_cparams(3),
        name="diff_attn",
    )(lam.reshape(1, 1).astype(F32), h, h, h, subln_g.reshape(1, dv).astype(F32))


def _conformer_kernel(a_ref, g_ref, ah_ref, gh_ref, w_ref, cb_ref, lg_ref, lb_ref, o_ref, cext_ref, conv_ref,
                      *, tiles_per_seq):
    i = pl.program_id(0)
    tr, width = a_ref.shape
    halo = ah_ref.shape[0]
    ksz = w_ref.shape[0]
    first = (i % tiles_per_seq) == 0

    glu_h = ah_ref[...].astype(F32) * jax.nn.sigmoid(gh_ref[...].astype(F32))
    cext_ref[0:halo, :] = jnp.where(first, 0.0, glu_h)
    cext_ref[halo:halo + tr, :] = a_ref[...].astype(F32) * jax.nn.sigmoid(g_ref[...].astype(F32))
    cext_ref[halo + tr:halo + tr + SUBLANES, :] = jnp.zeros((SUBLANES, width), F32)

    base = halo - (ksz - 1)
    win = CONV_RB + halo + SUBLANES

    def chunk(rc, carry):
        r0 = pl.multiple_of(rc * CONV_RB, CONV_RB)
        for c in range(width // LANES):
            lanes = slice(c * LANES, (c + 1) * LANES)
            wnd = cext_ref[pl.ds(r0, win), lanes]
            acc = jnp.zeros((CONV_RB, LANES), F32)
            for b in range(SUBLANES):
                shifted = wnd if b == 0 else pltpu.roll(wnd, win - b, axis=0)
                for a in range((halo + SUBLANES) // SUBLANES):
                    j = SUBLANES * a + b - base
                    if 0 <= j < ksz:
                        acc = acc + w_ref[j:j + 1, lanes] * shifted[SUBLANES * a:SUBLANES * a + CONV_RB]
            conv_ref[pl.ds(r0, CONV_RB), lanes] = acc + cb_ref[:, lanes]
        return carry

    lax.fori_loop(0, tr // CONV_RB, chunk, 0)
    y = _layer_norm(conv_ref[...], lg_ref[...], lb_ref[...])
    o_ref[...] = (y * jax.nn.sigmoid(y)).astype(o_ref.dtype)


def _conformer(h, conv_w, conv_b, ln_g, ln_b, *, seq, col_a, col_g):
    t = h.shape[0]
    ksz, width = conv_w.shape
    tr = min(TR_MIX, seq)
    halo = CONV_HALO
    assert seq % tr == 0 and tr % halo == 0 and ksz - 1 <= halo and tr % CONV_RB == 0
    rpb = tr // halo
    row = lambda v: v.reshape(1, width).astype(F32)
    return pl.pallas_call(
        functools.partial(_conformer_kernel, tiles_per_seq=seq // tr),
        grid=(t // tr,),
        in_specs=[
            pl.BlockSpec((tr, width), lambda i: (i, col_a)),
            pl.BlockSpec((tr, width), lambda i: (i, col_g)),
            pl.BlockSpec((halo, width), lambda i: (jnp.maximum(i * rpb - 1, 0), col_a)),
            pl.BlockSpec((halo, width), lambda i: (jnp.maximum(i * rpb - 1, 0), col_g)),
            pl.BlockSpec((ksz, width), lambda i: (0, 0)),
            pl.BlockSpec((1, width), lambda i: (0, 0)),
            pl.BlockSpec((1, width), lambda i: (0, 0)),
            pl.BlockSpec((1, width), lambda i: (0, 0)),
        ],
        out_specs=pl.BlockSpec((tr, width), lambda i: (i, 0)),
        out_shape=jax.ShapeDtypeStruct((t, width), BF16),
        scratch_shapes=[pltpu.VMEM((halo + tr + SUBLANES, width), F32), pltpu.VMEM((tr, width), F32)],
        compiler_params=_cparams(1),
        name="conformer_conv",
    )(h, h, h, h, conv_w.astype(F32), row(conv_b), row(ln_g), row(ln_b))


def _gelu_exact(x):
    return 0.5 * x * (1.0 + lax.erf(x * math.sqrt(0.5)))


def _sgu_kernel(u_ref, v_ref, gb_ref, gc_ref, xi_ref, gch_ref, xih_ref, lg_ref, lb_ref, sw_ref, sb_ref, cw_ref,
                o_ref, pext_ref, *, tiles_per_seq):
    i = pl.program_id(0)
    tr, width = u_ref.shape
    halo = gch_ref.shape[0]
    n_grp, chunk, _ = sw_ref.shape
    gdim = width // n_grp
    first = (i % tiles_per_seq) == 0

    vg = _layer_norm(_gelu_exact(v_ref[...].astype(F32)), lg_ref[...], lb_ref[...]).astype(BF16)
    trow = lax.broadcasted_iota(jnp.int32, (chunk, chunk), 0)
    tcol = lax.broadcasted_iota(jnp.int32, (chunk, chunk), 1)
    for g in range(n_grp):
        wg = jnp.where(tcol <= trow, sw_ref[g], 0.0).astype(BF16)
        for n in range(tr // chunk):
            rows = slice(n * chunk, (n + 1) * chunk)
            cols = slice(g * gdim, (g + 1) * gdim)
            sv = jnp.dot(wg, vg[rows, cols], preferred_element_type=F32) + sb_ref[g]
            o_ref[rows, cols] = (_gelu_exact(u_ref[rows, cols].astype(F32)) * sv).astype(o_ref.dtype)

    ph = gch_ref[...].astype(F32) * xih_ref[...].astype(F32)
    pext_ref[0:halo, :] = jnp.where(first, 0.0, ph)
    pext_ref[halo:halo + tr, :] = gc_ref[...].astype(F32) * xi_ref[...].astype(F32)
    pe = pext_ref[...]
    ksz = cw_ref.shape[0]
    conv = cw_ref[ksz - 1:ksz, :] * pe[halo:]
    for back in range(1, ksz):
        conv = conv + cw_ref[ksz - 1 - back:ksz - back, :] * pltpu.roll(pe, back, axis=0)[halo:]
    o_ref[:, width:2 * width] = (gb_ref[...].astype(F32) * conv).astype(o_ref.dtype)


def _sgu_shortconv(h, ln_g, ln_b, sgu_w, sgu_b, sc_w, *, seq):
    t = h.shape[0]
    width = ln_g.shape[-1]
    n_grp, chunk, _ = sgu_w.shape
    gdim = width // n_grp
    tr = min(TR_MIX, seq)
    halo = SC_HALO
    assert seq % tr == 0 and tr % chunk == 0 and tr % halo == 0 and sc_w.shape[0] - 1 <= halo
    rpb = tr // halo
    row = lambda v: v.reshape(1, width).astype(F32)
    bias = jnp.broadcast_to(sgu_b.astype(F32)[:, :, None], (n_grp, chunk, gdim))
    cur = lambda c: pl.BlockSpec((tr, width), lambda i: (i, c))
    prev = lambda c: pl.BlockSpec((halo, width), lambda i: (jnp.maximum(i * rpb - 1, 0), c))
    whole = lambda a: pl.BlockSpec(a.shape, lambda i: (0,) * a.ndim)
    small = [row(ln_g), row(ln_b), sgu_w.astype(F32), bias, sc_w.astype(F32)]
    return pl.pallas_call(
        functools.partial(_sgu_kernel, tiles_per_seq=seq // tr),
        grid=(t // tr,),
        in_specs=[cur(0), cur(1), cur(2), cur(3), cur(4), prev(3), prev(4)] + [whole(a) for a in small],
        out_specs=pl.BlockSpec((tr, 2 * width), lambda i: (i, 0)),
        out_shape=jax.ShapeDtypeStruct((t, 2 * width), BF16),
        scratch_shapes=[pltpu.VMEM((halo + tr, width), F32)],
        compiler_params=_cparams(1),
        name="sgu_shortconv",
    )(h, h, h, h, h, h, h, *small)


def _mix_cross_router_kernel(*refs, n_parts, alpha, scale, n_heads, n_groups, per_group):
    parts = refs[:n_parts]
    ws = refs[n_parts:2 * n_parts]
    (x_ref, g1_ref, b1_ref, wq_ref, k_ref, v_ref, wo_ref, g2_ref, b2_ref, wr_ref, br_ref,
     o_ref, r_ref) = refs[2 * n_parts:]
    half = x_ref.shape[0] // 2
    rows_a, rows_b = slice(0, half), slice(half, 2 * half)

    def out_projection(rows):
        acc = alpha * x_ref[rows, :]
        for p_ref, w_ref in zip(parts, ws):
            acc = acc + jnp.dot(p_ref[rows, :], w_ref[...], preferred_element_type=F32)
        return acc

    def norm_mix(acc):
        return _layer_norm(acc, g1_ref[...], b1_ref[...])

    def query(x1):
        return (jnp.dot(x1.astype(BF16), wq_ref[...], preferred_element_type=F32) * scale).astype(BF16)

    def attend(q):
        dh = q.shape[1] // n_heads
        outs = []
        for hh in range(n_heads):
            cols = slice(hh * dh, (hh + 1) * dh)
            s = lax.dot_general(q[:, cols], k_ref[:, cols], (((1,), (1,)), ((), ())), preferred_element_type=F32)
            e = jnp.exp(s - jnp.max(s, axis=-1, keepdims=True))
            pr = e / jnp.sum(e, axis=-1, keepdims=True)
            outs.append(jnp.dot(pr.astype(BF16), v_ref[:, cols], preferred_element_type=F32))
        return jnp.concatenate(outs, axis=-1).astype(BF16)

    def residual(x1, o):
        return alpha * x1 + jnp.dot(o, wo_ref[...], preferred_element_type=F32)

    def norm_mem(z, rows):
        y = _layer_norm(z, g2_ref[...], b2_ref[...])
        o_ref[rows, :] = y
        return y

    def route(y, rows):
        logits = jnp.dot(y.astype(BF16), wr_ref[...], preferred_element_type=F32) + br_ref[...]
        lane = lax.broadcasted_iota(jnp.int32, logits.shape, 1)
        far = jnp.int32(LANES)
        is_g = lane < n_groups
        gl = jnp.where(is_g, logits, -jnp.inf)
        gmax = jnp.max(gl, axis=-1, keepdims=True)
        grp = jnp.min(jnp.where(is_g & (gl == gmax), lane, far), axis=-1, keepdims=True)
        g_gate = 1.0 / jnp.sum(jnp.where(is_g, jnp.exp(gl - gmax), 0.0), axis=-1, keepdims=True)
        lo = n_groups + per_group * grp
        is_e = (lane >= lo) & (lane < lo + per_group)
        el = jnp.where(is_e, logits, -jnp.inf)
        v1 = jnp.max(el, axis=-1, keepdims=True)
        i1 = jnp.min(jnp.where(is_e & (el == v1), lane, far), axis=-1, keepdims=True)
        is_e2 = is_e & (lane != i1)
        el2 = jnp.where(is_e2, logits, -jnp.inf)
        v2 = jnp.max(el2, axis=-1, keepdims=True)
        i2 = jnp.min(jnp.where(is_e2 & (el2 == v2), lane, far), axis=-1, keepdims=True)
        e2 = jnp.exp(v2 - v1)
        w1 = g_gate / (1.0 + e2)
        w2 = g_gate * e2 / (1.0 + e2)
        r_ref[rows, :] = jnp.where(lane == 0, (i1 - n_groups).astype(F32),
                                   jnp.where(lane == 1, (i2 - n_groups).astype(F32),
                                             jnp.where(lane == 2, w1, jnp.where(lane == 3, w2, 0.0))))

    acc_a = out_projection(rows_a)
    acc_b = out_projection(rows_b)
    x1_a = norm_mix(acc_a)
    q_a = query(x1_a)
    x1_b = norm_mix(acc_b)
    o_a = attend(q_a)
    q_b = query(x1_b)
    z_a = residual(x1_a, o_a)
    o_b = attend(q_b)
    y_a = norm_mem(z_a, rows_a)
    z_b = residual(x1_b, o_b)
    route(y_a, rows_a)
    y_b = norm_mem(z_b, rows_b)
    route(y_b, rows_b)


def _mix_cross_router(parts, w_out_bf16, x, g1, b1, kv_bf16, wq_bf16, wo_bf16, g2, b2, wr_bf16, br,
                      *, seq, mem_len, alpha, per_group):
    t, d = x.shape
    cw = wq_bf16.shape[1]
    tm = min(TM_PROJ, seq)
    assert seq % tm == 0
    tps = seq // tm
    dh = cw // CROSS_HEADS
    n_exp = wr_bf16.shape[1]
    const = lambda shape: pl.BlockSpec(shape, lambda i: (0, 0), pipeline_mode=pl.Buffered(1))
    row = lambda v: v.reshape(1, d).astype(F32)
    in_specs, ws, off = [], [], 0
    for p in parts:
        kp = p.shape[1]
        in_specs.append(pl.BlockSpec((tm, kp), lambda i: (i, 0)))
        ws.append(w_out_bf16[off:off + kp])
        off += kp
    assert off == w_out_bf16.shape[0]
    in_specs += [const(w.shape) for w in ws]
    in_specs += [
        pl.BlockSpec((tm, d), lambda i: (i, 0)), const((1, d)), const((1, d)),
        const((d, cw)),
        pl.BlockSpec((mem_len, cw), lambda i: (i // tps, 0)),
        pl.BlockSpec((mem_len, cw), lambda i: (i // tps, 1)),
        const((cw, d)), const((1, d)), const((1, d)), const((d, n_exp)), const((1, n_exp)),
    ]
    return pl.pallas_call(
        functools.partial(_mix_cross_router_kernel, n_parts=len(parts), alpha=alpha, scale=dh ** -0.5,
                          n_heads=CROSS_HEADS, n_groups=N_GROUPS, per_group=per_group),
        grid=(t // tm,),
        in_specs=in_specs,
        out_specs=[pl.BlockSpec((tm, d), lambda i: (i, 0)), pl.BlockSpec((tm, LANES), lambda i: (i, 0))],
        out_shape=[jax.ShapeDtypeStruct((t, d), F32), jax.ShapeDtypeStruct((t, LANES), F32)],
        compiler_params=_cparams(1),
        name="mix_cross_router",
    )(*parts, *ws, x, row(g1), row(b1), wq_bf16, kv_bf16, kv_bf16, wo_bf16, row(g2), row(b2), wr_bf16, br)


def _moe_kernel(be_ref, nu_ref, idx0_ref, idx_ref, w1_ref, w3_ref, w2_ref,
                x_hbm, o_hbm, xb0, xb1, yb0, yb1, wb1, wb3, wb2, gsem, ssem):
    i = pl.program_id(0)
    n_used = nu_ref[0]
    tb = xb0.shape[0]
    xbufs, ybufs = (xb0, xb1), (yb0, yb1)

    def gather_row(tok, r, s):
        return pltpu.make_async_copy(x_hbm.at[pl.ds(tok, 1), :], xbufs[s].at[pl.ds(r, 1), :], gsem.at[s])

    def scatter_row(dst, r, s):
        return pltpu.make_async_copy(ybufs[s].at[pl.ds(r, 1), :], o_hbm.at[pl.ds(dst, 1), :], ssem.at[s])

    def wait_gather(s):
        pltpu.make_async_copy(x_hbm.at[pl.ds(0, tb), :], xbufs[s], gsem.at[s]).wait()

    def wait_scatter(s):
        pltpu.make_async_copy(ybufs[s], o_hbm.at[pl.ds(0, tb), :], ssem.at[s]).wait()

    def step(slot):
        other = 1 - slot

        @pl.when(i == 0)
        def _():
            def body(r, c):
                gather_row(idx0_ref[0, 0, r], r, 0).start()
                return c
            lax.fori_loop(0, tb, body, 0, unroll=8)
            yb1[...] = jnp.zeros(yb1.shape, F32)

        first_of_expert = jnp.logical_or(i == 0, be_ref[i] != be_ref[jnp.maximum(i - 1, 0)])

        @pl.when(first_of_expert)
        def _():
            wb1[...] = w1_ref[0, 0].astype(BF16)
            wb3[...] = w3_ref[0, 0].astype(BF16)
            wb2[...] = w2_ref[0, 0].astype(BF16)

        wait_gather(slot)
        for r in range(tb):
            gather_row(idx_ref[0, 0, r], r, other).start(priority=r % 2)
        for r in range(tb):
            scatter_row(idx_ref[0, 0, tb + r], r, other).start(priority=r % 2)
        xb = xbufs[slot][...].astype(BF16)
        h1 = jnp.dot(xb, wb1[...], preferred_element_type=F32)
        h3 = jnp.dot(xb, wb3[...], preferred_element_type=F32)
        hb = (h1 * jax.nn.sigmoid(h1) * h3).astype(BF16)
        ybufs[slot][...] = jnp.dot(hb, wb2[...], preferred_element_type=F32)
        wait_scatter(other)

    def drain(slot):
        other = 1 - slot
        wait_gather(slot)

        def body(r, c):
            scatter_row(idx_ref[0, 0, tb + r], r, other).start()
            return c
        lax.fori_loop(0, tb, body, 0, unroll=8)
        wait_scatter(other)

    for parity in range(2):
        @pl.when(jnp.logical_and(i < n_used, (i & 1) == parity))
        def _():
            step(parity)

        @pl.when(jnp.logical_and(i == n_used, (i & 1) == parity))
        def _():
            drain(parity)


def _moe_experts(x, blk_e, n_used, tok, dst, w1, w3, w2, *, layer, n_out_rows):
    t, d = x.shape
    nb = blk_e.shape[0]
    tb = tok.shape[0] // nb
    ff = w1.shape[-1]
    idx = jnp.concatenate([tok.reshape(nb, 1, tb), jnp.roll(dst.reshape(nb, 1, tb), 2, axis=0)], axis=-1)
    smem_blk = lambda f: pl.BlockSpec((1, 1, 2 * tb), f, memory_space=pltpu.SMEM)
    grid_spec = pltpu.PrefetchScalarGridSpec(
        num_scalar_prefetch=2,
        grid=(nb,),
        in_specs=[
            smem_blk(lambda i, be, nu: (0, 0, 0)),
            smem_blk(lambda i, be, nu: (jnp.minimum(i + 1, nb - 1), 0, 0)),
            pl.BlockSpec((1, 1, d, ff), lambda i, be, nu: (layer, be[i], 0, 0)),
            pl.BlockSpec((1, 1, d, ff), lambda i, be, nu: (layer, be[i], 0, 0)),
            pl.BlockSpec((1, 1, ff, d), lambda i, be, nu: (layer, be[i], 0, 0)),
            pl.BlockSpec(memory_space=pl.ANY),
        ],
        out_specs=pl.BlockSpec(memory_space=pl.ANY),
        scratch_shapes=[pltpu.VMEM((tb, d), F32), pltpu.VMEM((tb, d), F32),
                        pltpu.VMEM((tb, d), F32), pltpu.VMEM((tb, d), F32),
                        pltpu.VMEM((d, ff), BF16), pltpu.VMEM((d, ff), BF16), pltpu.VMEM((ff, d), BF16),
                        pltpu.SemaphoreType.DMA((2,)), pltpu.SemaphoreType.DMA((2,))],
    )
    return pl.pallas_call(
        _moe_kernel,
        grid_spec=grid_spec,
        out_shape=jax.ShapeDtypeStruct((n_out_rows, d), F32),
        compiler_params=_cparams(1),
        name="moe_experts",
    )(blk_e, n_used, idx, idx, w1, w3, w2, x)


def _moe_dispatch(route, *, n_experts, tb):
    t = route.shape[0]
    n_assign = TOP_K * t
    nb = n_assign // tb + n_experts + 1
    eid = route[:, :TOP_K].astype(jnp.int32).reshape(-1)
    assert n_experts * n_assign < 2 ** 31
    order = jnp.sort(eid * n_assign + jnp.arange(n_assign, dtype=jnp.int32)) % n_assign
    counts = jnp.sum((eid[:, None] == jnp.arange(n_experts, dtype=jnp.int32)[None, :]).astype(jnp.int32), axis=0)
    blocks = (counts + tb - 1) // tb
    bend = jnp.cumsum(blocks)
    start = jnp.cumsum(counts) - counts
    blk = jnp.arange(nb, dtype=jnp.int32)
    blk_e = jnp.minimum(jnp.sum((blk[:, None] >= bend[None, :]).astype(jnp.int32), axis=1), n_experts - 1)
    k_in_e = blk - (bend - blocks)[blk_e]
    n_valid = jnp.clip(counts[blk_e] - k_in_e * tb, 0, tb)
    r = jnp.arange(tb, dtype=jnp.int32)
    valid = r[None, :] < n_valid[:, None]
    src = jnp.where(valid, (start[blk_e] + k_in_e * tb)[:, None] + r[None, :], 0)
    a = order[src]
    a_tok = a // TOP_K
    tok = jnp.where(valid, a_tok, 0).reshape(-1)
    dst = jnp.where(valid, (a % TOP_K) * t + a_tok, TOP_K * t + r[None, :]).reshape(-1)
    return blk_e.astype(jnp.int32), bend[-1:].astype(jnp.int32), tok, dst


def _combine_ln_kernel(x_ref, y0_ref, y1_ref, r_ref, g_ref, b_ref, o_ref, *, alpha):
    w0 = r_ref[:, TOP_K:TOP_K + 1]
    w1 = r_ref[:, TOP_K + 1:TOP_K + 2]
    o_ref[...] = _layer_norm(alpha * x_ref[...] + w0 * y0_ref[...] + w1 * y1_ref[...], g_ref[...], b_ref[...])


def _combine_ln(x, y, route, g, b, *, alpha):
    t, d = x.shape
    tm = min(TM_PROJ, t)
    assert t % tm == 0
    nt = t // tm
    return pl.pallas_call(
        functools.partial(_combine_ln_kernel, alpha=alpha),
        grid=(nt,),
        in_specs=[pl.BlockSpec((tm, d), lambda i: (i, 0)),
                  pl.BlockSpec((tm, d), lambda i: (i, 0)),
                  pl.BlockSpec((tm, d), lambda i: (i + nt, 0)),
                  pl.BlockSpec((tm, LANES), lambda i: (i, 0)),
                  pl.BlockSpec((1, d), lambda i: (0, 0)), pl.BlockSpec((1, d), lambda i: (0, 0))],
        out_specs=pl.BlockSpec((tm, d), lambda i: (i, 0)),
        out_shape=jax.ShapeDtypeStruct((t, d), F32),
        compiler_params=_cparams(1),
        name="moe_combine_ln",
    )(x, y, y, route, g.reshape(1, d).astype(F32), b.reshape(1, d).astype(F32))


def _rope_head_perm(dh):
    r = dh // 8
    old = list(range(dh))
    rest = old[2 * r:]
    n_low = dh // 2 - r
    return jnp.asarray(old[:r] + rest[:n_low] + old[r:2 * r] + rest[n_low:], jnp.int32)


def _rope_tables(positions, dh):
    rope_dim = dh // 4
    half = rope_dim // 2
    assert dh == LANES
    inv_freq = ROPE_THETA ** (-jnp.arange(0, rope_dim, 2, dtype=F32) / rope_dim)
    ang = positions.astype(F32).reshape(-1, 1) * inv_freq[None, :]
    cos, sin = jnp.cos(ang), jnp.sin(ang)
    t = ang.shape[0]
    gap = dh // 2 - half
    c = jnp.concatenate([cos, jnp.ones((t, gap), F32), cos, jnp.ones((t, gap), F32)], axis=1)
    s = jnp.concatenate([-sin, jnp.zeros((t, gap), F32), sin, jnp.zeros((t, gap), F32)], axis=1)
    return c, s


def kernel(x, mem, positions, w_in, w_out, ln_mix_g, ln_mix_b, ln_mem_g, ln_mem_b, ln_ffn_g, ln_ffn_b, lam_q1, lam_k1, lam_q2, lam_k2, diff_subln_g, conv_w, conv_b, conv_ln_g, conv_ln_b, sgu_ln_g, sgu_ln_b, sgu_w, sgu_b, sc_w, mem_kv_w, xq_w, xo_w, rg_w, rg_b, re_w, re_b, e_w1, e_w3, e_w2):
    bsz, seq, d = x.shape
    t = bsz * seq
    depth = w_in.shape[0]
    mem_len = mem.shape[1]
    half = w_in.shape[2] // 5
    dh = lam_q1.shape[-1]
    n_experts = re_w.shape[-1]
    alpha = (2 * depth) ** 0.25
    assert half % TN_PROJ == 0 and diff_subln_g.shape[-1] == 2 * dh and half == 2 * DIFF_HEADS * dh

    xt = x.reshape(t, d)
    rope = _rope_tables(positions, dh)
    kv = _inproj(mem.reshape(bsz * mem_len, d), mem_kv_w.astype(BF16))

    for l in range(depth):
        j = l // 2
        w_l = w_in[l].astype(BF16)
        if l % 2 == 0:
            qk_cols = 2 * half
            col = jnp.arange(qk_cols, dtype=jnp.int32)
            col = (col // dh) * dh + _rope_head_perm(dh)[col % dh]
            w_l = jnp.concatenate([jnp.take(w_l[:, :qk_cols], col, axis=1), w_l[:, qk_cols:]], axis=1)
            h = _inproj(xt, w_l, rope, n_rope_tiles=2 * half // TN_PROJ, n_scale_tiles=half // TN_PROJ,
                        scale=dh ** -0.5 * math.log2(math.e))
            lam_init = 0.8 - 0.6 * math.exp(-0.3 * l)
            lam = (jnp.exp(jnp.sum(lam_q1[j].astype(F32) * lam_k1[j].astype(F32)))
                   - jnp.exp(jnp.sum(lam_q2[j].astype(F32) * lam_k2[j].astype(F32))) + lam_init)
            o = _diff_attention(h, lam, diff_subln_g[j], bsz=bsz, seq=seq, out_scale=1.0 - lam_init)
            c = _conformer(h, conv_w[j], conv_b[j], conv_ln_g[j], conv_ln_b[j], seq=seq, col_a=3, col_g=4)
            parts = [o, c]
        else:
            h = _inproj(xt, w_l)
            parts = [_sgu_shortconv(h, sgu_ln_g[j], sgu_ln_b[j], sgu_w[j], sgu_b[j], sc_w[j], seq=seq)]
        n_route = N_GROUPS + n_experts
        wr = jnp.concatenate([rg_w[l], re_w[l], jnp.zeros((d, LANES - n_route), F32)], axis=1).astype(BF16)
        br = jnp.concatenate([rg_b[l], re_b[l], jnp.zeros((LANES - n_route,), F32)]).reshape(1, LANES).astype(F32)
        xt, route = _mix_cross_router(parts, w_out[l].astype(BF16), xt, ln_mix_g[l], ln_mix_b[l], kv,
                                      xq_w[l].astype(BF16), xo_w[l].astype(BF16), ln_mem_g[l], ln_mem_b[l], wr, br,
                                      seq=seq, mem_len=mem_len, alpha=alpha, per_group=n_experts // N_GROUPS)

        blk_e, n_used, tok, dst = _moe_dispatch(route, n_experts=n_experts, tb=TB_MOE)
        y = _moe_experts(xt, blk_e, n_used, tok, dst, e_w1, e_w3, e_w2, layer=l, n_out_rows=TOP_K * t + TB_MOE)
        xt = _combine_ln(xt, y, route, ln_ffn_g[l], ln_ffn_b[l], alpha=alpha)
    return xt.reshape(bsz, seq, d)
```

```python
import functools
import math

import jax
import jax.numpy as jnp
from jax import lax
from jax.experimental import pallas as pl
from jax.experimental.pallas import tpu as pltpu

F32 = jnp.float32
BF16 = jnp.bfloat16

DIFF_HEADS = 4
ROPE_THETA = 500000.0
CROSS_HEADS = 4
N_GROUPS = 4
TOP_K = 2
LN_EPS = 1e-5

LANES = 128
SUBLANES = 8
NEG_BIG = -1e30
VMEM_LIMIT = 56 * 1024 * 1024

TM_PROJ = 512
TM_INPROJ = 1024
TN_PROJ = 1024
TQ_ATT = 1024
ATT_RC = 64
TR_MIX = 256
CONV_HALO = 32
CONV_RB = 32
SC_HALO = 16
TB_MOE = 256


def _cparams(n_axes):
    return pltpu.CompilerParams(dimension_semantics=("arbitrary",) * n_axes,
                                vmem_limit_bytes=VMEM_LIMIT)


def _layer_norm(y, g, b):
    mu = jnp.mean(y, axis=-1, keepdims=True)
    d = y - mu
    var = jnp.mean(d * d, axis=-1, keepdims=True)
    return d * lax.rsqrt(var + LN_EPS) * g + b


def _inproj_kernel(x_ref, w_ref, *rest, n_rope_tiles, n_scale_tiles, scale):
    if n_rope_tiles:
        c_ref, s_ref, o_ref, xb_ref = rest
    else:
        o_ref, xb_ref = rest
    j = pl.program_id(1)

    @pl.when(j == 0)
    def _():
        xb_ref[...] = x_ref[...].astype(BF16)

    def plain():
        o_ref[...] = jnp.dot(xb_ref[...], w_ref[...], preferred_element_type=F32).astype(o_ref.dtype)

    if not n_rope_tiles:
        plain()
        return

    pl.when(j >= n_rope_tiles)(plain)

    @pl.when(j < n_rope_tiles)
    def _():
        q_scale = jnp.where(j < n_scale_tiles, scale, 1.0).astype(F32)
        tm = o_ref.shape[0]
        quarters = [slice(i * (tm // 4), (i + 1) * (tm // 4)) for i in range(4)]

        def product(rows):
            return jnp.dot(xb_ref[rows, :], w_ref[...], preferred_element_type=F32)

        def rotary(rows, acc):
            a = acc * q_scale
            c, s = c_ref[rows, :], s_ref[rows, :]
            for g in range(a.shape[1] // LANES):
                ag = a[:, g * LANES:(g + 1) * LANES]
                og = ag * c + pltpu.roll(ag, LANES // 2, axis=1) * s
                o_ref[rows, g * LANES:(g + 1) * LANES] = og.astype(o_ref.dtype)

        accs = [product(quarters[0])]
        for i in range(4):
            if i + 1 < 4:
                accs.append(product(quarters[i + 1]))
            rotary(quarters[i], accs[i])


def _inproj(x, w_bf16, rope=None, *, n_rope_tiles=0, n_scale_tiles=0, scale=1.0, out_dtype=BF16):
    m, k = x.shape
    n = w_bf16.shape[1]
    tm = min(TM_INPROJ, m)
    tn = min(TN_PROJ, n)
    assert m % tm == 0 and n % tn == 0
    in_specs = [pl.BlockSpec((tm, k), lambda i, j: (i, 0)),
                pl.BlockSpec((k, tn), lambda i, j: (0, j))]
    args = [x, w_bf16]
    if n_rope_tiles:
        in_specs += [pl.BlockSpec((tm, LANES), lambda i, j: (i, 0))] * 2
        args += list(rope)
    return pl.pallas_call(
        functools.partial(_inproj_kernel, n_rope_tiles=n_rope_tiles, n_scale_tiles=n_scale_tiles, scale=scale),
        grid=(m // tm, n // tn),
        in_specs=in_specs,
        out_specs=pl.BlockSpec((tm, tn), lambda i, j: (i, j)),
        out_shape=jax.ShapeDtypeStruct((m, n), out_dtype),
        scratch_shapes=[pltpu.VMEM((tm, k), BF16)],
        compiler_params=_cparams(2),
        name="inproj_rope" if n_rope_tiles else "inproj",
    )(*args)


def _diff_attn_kernel(lam_ref, q_ref, k_ref, v_ref, g_ref, o_ref, s_a, s_b, p_a, p_b, a_a, a_b,
                      m_ref, l_ref, acc_ref, *, dh, out_scale):
    qi = pl.program_id(2)
    tq = q_ref.shape[0]
    tk = s_a.shape[2]
    dv = v_ref.shape[1]
    ngrp = tk // LANES
    m_ref[...] = jnp.full(m_ref.shape, NEG_BIG, F32)
    l_ref[...] = jnp.zeros(l_ref.shape, F32)
    acc_ref[...] = jnp.zeros(acc_ref.shape, F32)

    def scores(j, s_ref):
        r0 = pl.multiple_of(j * tk, tk)
        for mi in range(2):
            cols = slice(mi * dh, (mi + 1) * dh)
            s_ref[mi] = lax.dot_general(q_ref[:, cols], k_ref[pl.ds(r0, tk), cols], (((1,), (1,)), ((), ())),
                                        preferred_element_type=F32)

    def softmax(s_ref, p_ref, a_ref, mask_shift=None):
        for mi in range(2):
            for rc in range(tq // ATT_RC):
                rows = slice(rc * ATT_RC, (rc + 1) * ATT_RC)
                groups = [s_ref[mi, rows, g * LANES:(g + 1) * LANES] for g in range(ngrp)]
                if mask_shift is not None:
                    row = lax.broadcasted_iota(jnp.int32, (ATT_RC, LANES), 0) + rc * ATT_RC
                    col = lax.broadcasted_iota(jnp.int32, (ATT_RC, LANES), 1) + mask_shift
                    groups = [jnp.where(col + g * LANES <= row, sg, NEG_BIG) for g, sg in enumerate(groups)]
                mx = functools.reduce(jnp.maximum, groups)
                m_prev = m_ref[mi, rows]
                m_new = jnp.maximum(m_prev, jnp.max(mx, axis=1, keepdims=True))
                alpha = jnp.exp2(m_prev - m_new)
                lsum = None
                for g, sg in enumerate(groups):
                    pg = jnp.exp2(sg - m_new)
                    lsum = pg if lsum is None else lsum + pg
                    p_ref[mi, rows, g * LANES:(g + 1) * LANES] = pg.astype(BF16)
                l_ref[mi, rows] = alpha * l_ref[mi, rows] + lsum
                m_ref[mi, rows] = m_new
                a_ref[mi, rows] = alpha

    def weighted_values(j, p_ref, a_ref):
        r0 = pl.multiple_of(j * tk, tk)
        v = v_ref[pl.ds(r0, tk), :]
        for mi in range(2):
            pv = jnp.dot(p_ref[mi], v, preferred_element_type=F32)
            alpha = a_ref[mi]
            acc_ref[mi] = jnp.concatenate(
                [acc_ref[mi, :, c * LANES:(c + 1) * LANES] * alpha for c in range(dv // LANES)], axis=1) + pv

    scores(0, s_a)

    def pair(u, carry):
        j = 2 * u
        scores(j + 1, s_b)
        softmax(s_a, p_a, a_a)
        weighted_values(j, p_a, a_a)
        scores(j + 2, s_a)
        softmax(s_b, p_b, a_b)
        weighted_values(j + 1, p_b, a_b)
        return carry

    lax.fori_loop(0, qi, pair, 0)
    scores(2 * qi + 1, s_b)
    softmax(s_a, p_a, a_a, mask_shift=0)
    weighted_values(2 * qi, p_a, a_a)
    softmax(s_b, p_b, a_b, mask_shift=tk)
    weighted_values(2 * qi + 1, p_b, a_b)

    lam = lam_ref[0, 0]
    l0 = jnp.sum(l_ref[0], axis=1, keepdims=True)
    l1 = jnp.sum(l_ref[1], axis=1, keepdims=True)
    o = acc_ref[0] / l0 - lam * (acc_ref[1] / l1)
    ms = jnp.mean(o * o, axis=-1, keepdims=True)
    o_ref[...] = (o * lax.rsqrt(ms + LN_EPS) * g_ref[...] * out_scale).astype(o_ref.dtype)


def _diff_attention(h, lam, subln_g, *, bsz, seq, out_scale):
    t = h.shape[0]
    dv = subln_g.shape[-1]
    dh = dv // 2
    tq = min(TQ_ATT, seq)
    tk = tq // 2
    nq = seq // tq
    assert seq % tq == 0 and tq % (2 * ATT_RC) == 0
    hd_n = DIFF_HEADS
    resident = lambda c0: pl.BlockSpec((seq, dv), lambda b, hd, qi: (b, c0 + hd), pipeline_mode=pl.Buffered(1))
    return pl.pallas_call(
        functools.partial(_diff_attn_kernel, dh=dh, out_scale=out_scale),
        grid=(bsz, hd_n, nq),
        in_specs=[
            pl.BlockSpec(memory_space=pltpu.SMEM),
            pl.BlockSpec((tq, dv), lambda b, hd, qi: (b * nq + qi, hd)),
            resident(hd_n), resident(2 * hd_n),
            pl.BlockSpec((1, dv), lambda b, hd, qi: (0, 0)),
        ],
        out_specs=pl.BlockSpec((tq, dv), lambda b, hd, qi: (b * nq + qi, hd)),
        out_shape=jax.ShapeDtypeStruct((t, hd_n * dv), BF16),
        scratch_shapes=[pltpu.VMEM((2, tq, tk), F32), pltpu.VMEM((2, tq, tk), F32),
                        pltpu.VMEM((2, tq, tk), BF16), pltpu.VMEM((2, tq, tk), BF16),
                        pltpu.VMEM((2, tq, LANES), F32), pltpu.VMEM((2, tq, LANES), F32),
                        pltpu.VMEM((2, tq, LANES), F32), pltpu.VMEM((2, tq, LANES), F32),
                        pltpu.VMEM((2, tq, dv), F32)],
        compiler_params=_cparams(3),
        name="diff_attn",
    )(lam.reshape(1, 1).astype(F32), h, h, h, subln_g.reshape(1, dv).astype(F32))


def _conformer_kernel(a_ref, g_ref, ah_ref, gh_ref, w_ref, cb_ref, lg_ref, lb_ref, o_ref, cext_ref, conv_ref,
                      *, tiles_per_seq):
    i = pl.program_id(0)
    tr, width = a_ref.shape
    halo = ah_ref.shape[0]
    ksz = w_ref.shape[0]
    first = (i % tiles_per_seq) == 0

    glu_h = ah_ref[...].astype(F32) * jax.nn.sigmoid(gh_ref[...].astype(F32))
    cext_ref[0:halo, :] = jnp.where(first, 0.0, glu_h)
    cext_ref[halo:halo + tr, :] = a_ref[...].astype(F32) * jax.nn.sigmoid(g_ref[...].astype(F32))
    cext_ref[halo + tr:halo + tr + SUBLANES, :] = jnp.zeros((SUBLANES, width), F32)

    base = halo - (ksz - 1)
    win = CONV_RB + halo + SUBLANES

    def chunk(rc, carry):
        r0 = pl.multiple_of(rc * CONV_RB, CONV_RB)
        for c in range(width // LANES):
            lanes = slice(c * LANES, (c + 1) * LANES)
            wnd = cext_ref[pl.ds(r0, win), lanes]
            acc = jnp.zeros((CONV_RB, LANES), F32)
            for b in range(SUBLANES):
                shifted = wnd if b == 0 else pltpu.roll(wnd, win - b, axis=0)
                for a in range((halo + SUBLANES) // SUBLANES):
                    j = SUBLANES * a + b - base
                    if 0 <= j < ksz:
                        acc = acc + w_ref[j:j + 1, lanes] * shifted[SUBLANES * a:SUBLANES * a + CONV_RB]
            conv_ref[pl.ds(r0, CONV_RB), lanes] = acc + cb_ref[:, lanes]
        return carry

    lax.fori_loop(0, tr // CONV_RB, chunk, 0)
    y = _layer_norm(conv_ref[...], lg_ref[...], lb_ref[...])
    o_ref[...] = (y * jax.nn.sigmoid(y)).astype(o_ref.dtype)


def _conformer(h, conv_w, conv_b, ln_g, ln_b, *, seq, col_a, col_g):
    t = h.shape[0]
    ksz, width = conv_w.shape
    tr = min(TR_MIX, seq)
    halo = CONV_HALO
    assert seq % tr == 0 and tr % halo == 0 and ksz - 1 <= halo and tr % CONV_RB == 0
    rpb = tr // halo
    row = lambda v: v.reshape(1, width).astype(F32)
    return pl.pallas_call(
        functools.partial(_conformer_kernel, tiles_per_seq=seq // tr),
        grid=(t // tr,),
        in_specs=[
            pl.BlockSpec((tr, width), lambda i: (i, col_a)),
            pl.BlockSpec((tr, width), lambda i: (i, col_g)),
            pl.BlockSpec((halo, width), lambda i: (jnp.maximum(i * rpb - 1, 0), col_a)),
            pl.BlockSpec((halo, width), lambda i: (jnp.maximum(i * rpb - 1, 0), col_g)),
            pl.BlockSpec((ksz, width), lambda i: (0, 0)),
            pl.BlockSpec((1, width), lambda i: (0, 0)),
            pl.BlockSpec((1, width), lambda i: (0, 0)),
            pl.BlockSpec((1, width), lambda i: (0, 0)),
        ],
        out_specs=pl.BlockSpec((tr, width), lambda i: (i, 0)),
        out_shape=jax.ShapeDtypeStruct((t, width), BF16),
        scratch_shapes=[pltpu.VMEM((halo + tr + SUBLANES, width), F32), pltpu.VMEM((tr, width), F32)],
        compiler_params=_cparams(1),
        name="conformer_conv",
    )(h, h, h, h, conv_w.astype(F32), row(conv_b), row(ln_g), row(ln_b))


def _gelu_exact(x):
    return 0.5 * x * (1.0 + lax.erf(x * math.sqrt(0.5)))


def _sgu_kernel(u_ref, v_ref, gb_ref, gc_ref, xi_ref, gch_ref, xih_ref, lg_ref, lb_ref, sw_ref, sb_ref, cw_ref,
                o_ref, pext_ref, *, tiles_per_seq):
    i = pl.program_id(0)
    tr, width = u_ref.shape
    halo = gch_ref.shape[0]
    n_grp, chunk, _ = sw_ref.shape
    gdim = width // n_grp
    first = (i % tiles_per_seq) == 0

    vg = _layer_norm(_gelu_exact(v_ref[...].astype(F32)), lg_ref[...], lb_ref[...]).astype(BF16)
    trow = lax.broadcasted_iota(jnp.int32, (chunk, chunk), 0)
    tcol = lax.broadcasted_iota(jnp.int32, (chunk, chunk), 1)
    for g in range(n_grp):
        wg = jnp.where(tcol <= trow, sw_ref[g], 0.0).astype(BF16)
        for n in range(tr // chunk):
            rows = slice(n * chunk, (n + 1) * chunk)
            cols = slice(g * gdim, (g + 1) * gdim)
            sv = jnp.dot(wg, vg[rows, cols], preferred_element_type=F32) + sb_ref[g]
            o_ref[rows, cols] = (_gelu_exact(u_ref[rows, cols].astype(F32)) * sv).astype(o_ref.dtype)

    ph = gch_ref[...].astype(F32) * xih_ref[...].astype(F32)
    pext_ref[0:halo, :] = jnp.where(first, 0.0, ph)
    pext_ref[halo:halo + tr, :] = gc_ref[...].astype(F32) * xi_ref[...].astype(F32)
    pe = pext_ref[...]
    ksz = cw_ref.shape[0]
    conv = cw_ref[ksz - 1:ksz, :] * pe[halo:]
    for back in range(1, ksz):
        conv = conv + cw_ref[ksz - 1 - back:ksz - back, :] * pltpu.roll(pe, back, axis=0)[halo:]
    o_ref[:, width:2 * width] = (gb_ref[...].astype(F32) * conv).astype(o_ref.dtype)


def _sgu_shortconv(h, ln_g, ln_b, sgu_w, sgu_b, sc_w, *, seq):
    t = h.shape[0]
    width = ln_g.shape[-1]
    n_grp, chunk, _ = sgu_w.shape
    gdim = width // n_grp
    tr = min(TR_MIX, seq)
    halo = SC_HALO
    assert seq % tr == 0 and tr % chunk == 0 and tr % halo == 0 and sc_w.shape[0] - 1 <= halo
    rpb = tr // halo
    row = lambda v: v.reshape(1, width).astype(F32)
    bias = jnp.broadcast_to(sgu_b.astype(F32)[:, :, None], (n_grp, chunk, gdim))
    cur = lambda c: pl.BlockSpec((tr, width), lambda i: (i, c))
    prev = lambda c: pl.BlockSpec((halo, width), lambda i: (jnp.maximum(i * rpb - 1, 0), c))
    whole = lambda a: pl.BlockSpec(a.shape, lambda i: (0,) * a.ndim)
    small = [row(ln_g), row(ln_b), sgu_w.astype(F32), bias, sc_w.astype(F32)]
    return pl.pallas_call(
        functools.partial(_sgu_kernel, tiles_per_seq=seq // tr),
        grid=(t // tr,),
        in_specs=[cur(0), cur(1), cur(2), cur(3), cur(4), prev(3), prev(4)] + [whole(a) for a in small],
        out_specs=pl.BlockSpec((tr, 2 * width), lambda i: (i, 0)),
        out_shape=jax.ShapeDtypeStruct((t, 2 * width), BF16),
        scratch_shapes=[pltpu.VMEM((halo + tr, width), F32)],
        compiler_params=_cparams(1),
        name="sgu_shortconv",
    )(h, h, h, h, h, h, h, *small)


def _mix_cross_router_kernel(*refs, n_parts, alpha, scale, n_heads, n_groups, per_group):
    parts = refs[:n_parts]
    ws = refs[n_parts:2 * n_parts]
    (x_ref, g1_ref, b1_ref, wq_ref, k_ref, v_ref, wo_ref, g2_ref, b2_ref, wr_ref, br_ref,
     o_ref, r_ref) = refs[2 * n_parts:]
    half = x_ref.shape[0] // 2
    rows_a, rows_b = slice(0, half), slice(half, 2 * half)

    def out_projection(rows):
        acc = alpha * x_ref[rows, :]
        for p_ref, w_ref in zip(parts, ws):
            acc = acc + jnp.dot(p_ref[rows, :], w_ref[...], preferred_element_type=F32)
        return acc

    def norm_mix(acc):
        return _layer_norm(acc, g1_ref[...], b1_ref[...])

    def query(x1):
        return (jnp.dot(x1.astype(BF16), wq_ref[...], preferred_element_type=F32) * scale).astype(BF16)

    def attend(q):
        dh = q.shape[1] // n_heads
        outs = []
        for hh in range(n_heads):
            cols = slice(hh * dh, (hh + 1) * dh)
            s = lax.dot_general(q[:, cols], k_ref[:, cols], (((1,), (1,)), ((), ())), preferred_element_type=F32)
            e = jnp.exp(s - jnp.max(s, axis=-1, keepdims=True))
            pr = e / jnp.sum(e, axis=-1, keepdims=True)
            outs.append(jnp.dot(pr.astype(BF16), v_ref[:, cols], preferred_element_type=F32))
        return jnp.concatenate(outs, axis=-1).astype(BF16)

    def residual(x1, o):
        return alpha * x1 + jnp.dot(o, wo_ref[...], preferred_element_type=F32)

    def norm_mem(z, rows):
        y = _layer_norm(z, g2_ref[...], b2_ref[...])
        o_ref[rows, :] = y
        return y

    def route(y, rows):
        logits = jnp.dot(y.astype(BF16), wr_ref[...], preferred_element_type=F32) + br_ref[...]
        lane = lax.broadcasted_iota(jnp.int32, logits.shape, 1)
        far = jnp.int32(LANES)
        is_g = lane < n_groups
        gl = jnp.where(is_g, logits, -jnp.inf)
        gmax = jnp.max(gl, axis=-1, keepdims=True)
        grp = jnp.min(jnp.where(is_g & (gl == gmax), lane, far), axis=-1, keepdims=True)
        g_gate = 1.0 / jnp.sum(jnp.where(is_g, jnp.exp(gl - gmax), 0.0), axis=-1, keepdims=True)
        lo = n_groups + per_group * grp
        is_e = (lane >= lo) & (lane < lo + per_group)
        el = jnp.where(is_e, logits, -jnp.inf)
        v1 = jnp.max(el, axis=-1, keepdims=True)
        i1 = jnp.min(jnp.where(is_e & (el == v1), lane, far), axis=-1, keepdims=True)
        is_e2 = is_e & (lane != i1)
        el2 = jnp.where(is_e2, logits, -jnp.inf)
        v2 = jnp.max(el2, axis=-1, keepdims=True)
        i2 = jnp.min(jnp.where(is_e2 & (el2 == v2), lane, far), axis=-1, keepdims=True)
        e2 = jnp.exp(v2 - v1)
        w1 = g_gate / (1.0 + e2)
        w2 = g_gate * e2 / (1.0 + e2)
        r_ref[rows, :] = jnp.where(lane == 0, (i1 - n_groups).astype(F32),
                                   jnp.where(lane == 1, (i2 - n_groups).astype(F32),
                                             jnp.where(lane == 2, w1, jnp.where(lane == 3, w2, 0.0))))

    acc_a = out_projection(rows_a)
    acc_b = out_projection(rows_b)
    x1_a = norm_mix(acc_a)
    q_a = query(x1_a)
    x1_b = norm_mix(acc_b)
    o_a = attend(q_a)
    q_b = query(x1_b)
    z_a = residual(x1_a, o_a)
    o_b = attend(q_b)
    y_a = norm_mem(z_a, rows_a)
    z_b = residual(x1_b, o_b)
    route(y_a, rows_a)
    y_b = norm_mem(z_b, rows_b)
    route(y_b, rows_b)


def _mix_cross_router(parts, w_out_bf16, x, g1, b1, kv_bf16, wq_bf16, wo_bf16, g2, b2, wr_bf16, br,
                      *, seq, mem_len, alpha, per_group):
    t, d = x.shape
    cw = wq_bf16.shape[1]
    tm = min(TM_PROJ, seq)
    assert seq % tm == 0
    tps = seq // tm
    dh = cw // CROSS_HEADS
    n_exp = wr_bf16.shape[1]
    const = lambda shape: pl.BlockSpec(shape, lambda i: (0, 0), pipeline_mode=pl.Buffered(1))
    row = lambda v: v.reshape(1, d).astype(F32)
    in_specs, ws, off = [], [], 0
    for p in parts:
        kp = p.shape[1]
        in_specs.append(pl.BlockSpec((tm, kp), lambda i: (i, 0)))
        ws.append(w_out_bf16[off:off + kp])
        off += kp
    assert off == w_out_bf16.shape[0]
    in_specs += [const(w.shape) for w in ws]
    in_specs += [
        pl.BlockSpec((tm, d), lambda i: (i, 0)), const((1, d)), const((1, d)),
        const((d, cw)),
        pl.BlockSpec((mem_len, cw), lambda i: (i // tps, 0)),
        pl.BlockSpec((mem_len, cw), lambda i: (i // tps, 1)),
        const((cw, d)), const((1, d)), const((1, d)), const((d, n_exp)), const((1, n_exp)),
    ]
    return pl.pallas_call(
        functools.partial(_mix_cross_router_kernel, n_parts=len(parts), alpha=alpha, scale=dh ** -0.5,
                          n_heads=CROSS_HEADS, n_groups=N_GROUPS, per_group=per_group),
        grid=(t // tm,),
        in_specs=in_specs,
        out_specs=[pl.BlockSpec((tm, d), lambda i: (i, 0)), pl.BlockSpec((tm, LANES), lambda i: (i, 0))],
        out_shape=[jax.ShapeDtypeStruct((t, d), F32), jax.ShapeDtypeStruct((t, LANES), F32)],
        compiler_params=_cparams(1),
        name="mix_cross_router",
    )(*parts, *ws, x, row(g1), row(b1), wq_bf16, kv_bf16, kv_bf16, wo_bf16, row(g2), row(b2), wr_bf16, br)


def _moe_kernel(be_ref, nu_ref, idx0_ref, idx_ref, w1_ref, w3_ref, w2_ref,
                x_hbm, o_hbm, xb0, xb1, yb0, yb1, wb1, wb3, wb2, gsem, ssem):
    i = pl.program_id(0)
    n_used = nu_ref[0]
    tb = xb0.shape[0]
    xbufs, ybufs = (xb0, xb1), (yb0, yb1)

    def gather_row(tok, r, s):
        return pltpu.make_async_copy(x_hbm.at[pl.ds(tok, 1), :], xbufs[s].at[pl.ds(r, 1), :], gsem.at[s])

    def scatter_row(dst, r, s):
        return pltpu.make_async_copy(ybufs[s].at[pl.ds(r, 1), :], o_hbm.at[pl.ds(dst, 1), :], ssem.at[s])

    def wait_gather(s):
        pltpu.make_async_copy(x_hbm.at[pl.ds(0, tb), :], xbufs[s], gsem.at[s]).wait()

    def wait_scatter(s):
        pltpu.make_async_copy(ybufs[s], o_hbm.at[pl.ds(0, tb), :], ssem.at[s]).wait()

    def step(slot):
        other = 1 - slot

        @pl.when(i == 0)
        def _():
            def body(r, c):
                gather_row(idx0_ref[0, 0, r], r, 0).start()
                return c
            lax.fori_loop(0, tb, body, 0, unroll=8)
            yb1[...] = jnp.zeros(yb1.shape, F32)

        first_of_expert = jnp.logical_or(i == 0, be_ref[i] != be_ref[jnp.maximum(i - 1, 0)])

        @pl.when(first_of_expert)
        def _():
            wb1[...] = w1_ref[0, 0].astype(BF16)
            wb3[...] = w3_ref[0, 0].astype(BF16)
            wb2[...] = w2_ref[0, 0].astype(BF16)

        wait_gather(slot)
        for r in range(tb):
            gather_row(idx_ref[0, 0, r], r, other).start()
        for r in range(tb):
            scatter_row(idx_ref[0, 0, tb + r], r, other).start()
        xb = xbufs[slot][...].astype(BF16)
        h1 = jnp.dot(xb, wb1[...], preferred_element_type=F32)
        h3 = jnp.dot(xb, wb3[...], preferred_element_type=F32)
        hb = (h1 * jax.nn.sigmoid(h1) * h3).astype(BF16)
        ybufs[slot][...] = jnp.dot(hb, wb2[...], preferred_element_type=F32)
        wait_scatter(other)

    def drain(slot):
        other = 1 - slot
        wait_gather(slot)

        def body(r, c):
            scatter_row(idx_ref[0, 0, tb + r], r, other).start()
            return c
        lax.fori_loop(0, tb, body, 0, unroll=8)
        wait_scatter(other)

    for parity in range(2):
        @pl.when(jnp.logical_and(i < n_used, (i & 1) == parity))
        def _():
            step(parity)

        @pl.when(jnp.logical_and(i == n_used, (i & 1) == parity))
        def _():
            drain(parity)


def _moe_experts(x, blk_e, n_used, tok, dst, w1, w3, w2, *, layer, n_out_rows):
    t, d = x.shape
    nb = blk_e.shape[0]
    tb = tok.shape[0] // nb
    ff = w1.shape[-1]
    idx = jnp.concatenate([tok.reshape(nb, 1, tb), jnp.roll(dst.reshape(nb, 1, tb), 2, axis=0)], axis=-1)
    smem_blk = lambda f: pl.BlockSpec((1, 1, 2 * tb), f, memory_space=pltpu.SMEM)
    grid_spec = pltpu.PrefetchScalarGridSpec(
        num_scalar_prefetch=2,
        grid=(nb,),
        in_specs=[
            smem_blk(lambda i, be, nu: (0, 0, 0)),
            smem_blk(lambda i, be, nu: (jnp.minimum(i + 1, nb - 1), 0, 0)),
            pl.BlockSpec((1, 1, d, ff), lambda i, be, nu: (layer, be[i], 0, 0)),
            pl.BlockSpec((1, 1, d, ff), lambda i, be, nu: (layer, be[i], 0, 0)),
            pl.BlockSpec((1, 1, ff, d), lambda i, be, nu: (layer, be[i], 0, 0)),
            pl.BlockSpec(memory_space=pl.ANY),
        ],
        out_specs=pl.BlockSpec(memory_space=pl.ANY),
        scratch_shapes=[pltpu.VMEM((tb, d), F32), pltpu.VMEM((tb, d), F32),
                        pltpu.VMEM((tb, d), F32), pltpu.VMEM((tb, d), F32),
                        pltpu.VMEM((d, ff), BF16), pltpu.VMEM((d, ff), BF16), pltpu.VMEM((ff, d), BF16),
                        pltpu.SemaphoreType.DMA((2,)), pltpu.SemaphoreType.DMA((2,))],
    )
    return pl.pallas_call(
        _moe_kernel,
        grid_spec=grid_spec,
        out_shape=jax.ShapeDtypeStruct((n_out_rows, d), F32),
        compiler_params=_cparams(1),
        name="moe_experts",
    )(blk_e, n_used, idx, idx, w1, w3, w2, x)


def _moe_dispatch(route, *, n_experts, tb):
    t = route.shape[0]
    n_assign = TOP_K * t
    nb = n_assign // tb + n_experts + 1
    eid = route[:, :TOP_K].astype(jnp.int32).reshape(-1)
    assert n_experts * n_assign < 2 ** 31
    order = jnp.sort(eid * n_assign + jnp.arange(n_assign, dtype=jnp.int32)) % n_assign
    counts = jnp.sum((eid[:, None] == jnp.arange(n_experts, dtype=jnp.int32)[None, :]).astype(jnp.int32), axis=0)
    blocks = (counts + tb - 1) // tb
    bend = jnp.cumsum(blocks)
    start = jnp.cumsum(counts) - counts
    blk = jnp.arange(nb, dtype=jnp.int32)
    blk_e = jnp.minimum(jnp.sum((blk[:, None] >= bend[None, :]).astype(jnp.int32), axis=1), n_experts - 1)
    k_in_e = blk - (bend - blocks)[blk_e]
    n_valid = jnp.clip(counts[blk_e] - k_in_e * tb, 0, tb)
    r = jnp.arange(tb, dtype=jnp.int32)
    valid = r[None, :] < n_valid[:, None]
    src = jnp.where(valid, (start[blk_e] + k_in_e * tb)[:, None] + r[None, :], 0)
    a = order[src]
    a_tok = a // TOP_K
    tok = jnp.where(valid, a_tok, 0).reshape(-1)
    dst = jnp.where(valid, (a % TOP_K) * t + a_tok, TOP_K * t + r[None, :]).reshape(-1)
    return blk_e.astype(jnp.int32), bend[-1:].astype(jnp.int32), tok, dst


def _combine_ln_kernel(x_ref, y0_ref, y1_ref, r_ref, g_ref, b_ref, o_ref, *, alpha):
    w0 = r_ref[:, TOP_K:TOP_K + 1]
    w1 = r_ref[:, TOP_K + 1:TOP_K + 2]
    o_ref[...] = _layer_norm(alpha * x_ref[...] + w0 * y0_ref[...] + w1 * y1_ref[...], g_ref[...], b_ref[...])


def _combine_ln(x, y, route, g, b, *, alpha):
    t, d = x.shape
    tm = min(TM_PROJ, t)
    assert t % tm == 0
    nt = t // tm
    return pl.pallas_call(
        functools.partial(_combine_ln_kernel, alpha=alpha),
        grid=(nt,),
        in_specs=[pl.BlockSpec((tm, d), lambda i: (i, 0)),
                  pl.BlockSpec((tm, d), lambda i: (i, 0)),
                  pl.BlockSpec((tm, d), lambda i: (i + nt, 0)),
                  pl.BlockSpec((tm, LANES), lambda i: (i, 0)),
                  pl.BlockSpec((1, d), lambda i: (0, 0)), pl.BlockSpec((1, d), lambda i: (0, 0))],
        out_specs=pl.BlockSpec((tm, d), lambda i: (i, 0)),
        out_shape=jax.ShapeDtypeStruct((t, d), F32),
        compiler_params=_cparams(1),
        name="moe_combine_ln",
    )(x, y, y, route, g.reshape(1, d).astype(F32), b.reshape(1, d).astype(F32))


def _rope_head_perm(dh):
    r = dh // 8
    old = list(range(dh))
    rest = old[2 * r:]
    n_low = dh // 2 - r
    return jnp.asarray(old[:r] + rest[:n_low] + old[r:2 * r] + rest[n_low:], jnp.int32)


def _rope_tables(positions, dh):
    rope_dim = dh // 4
    half = rope_dim // 2
    assert dh == LANES
    inv_freq = ROPE_THETA ** (-jnp.arange(0, rope_dim, 2, dtype=F32) / rope_dim)
    ang = positions.astype(F32).reshape(-1, 1) * inv_freq[None, :]
    cos, sin = jnp.cos(ang), jnp.sin(ang)
    t = ang.shape[0]
    gap = dh // 2 - half
    c = jnp.concatenate([cos, jnp.ones((t, gap), F32), cos, jnp.ones((t, gap), F32)], axis=1)
    s = jnp.concatenate([-sin, jnp.zeros((t, gap), F32), sin, jnp.zeros((t, gap), F32)], axis=1)
    return c, s


def kernel(x, mem, positions, w_in, w_out, ln_mix_g, ln_mix_b, ln_mem_g, ln_mem_b, ln_ffn_g, ln_ffn_b, lam_q1, lam_k1, lam_q2, lam_k2, diff_subln_g, conv_w, conv_b, conv_ln_g, conv_ln_b, sgu_ln_g, sgu_ln_b, sgu_w, sgu_b, sc_w, mem_kv_w, xq_w, xo_w, rg_w, rg_b, re_w, re_b, e_w1, e_w3, e_w2):
    bsz, seq, d = x.shape
    t = bsz * seq
    depth = w_in.shape[0]
    mem_len = mem.shape[1]
    half = w_in.shape[2] // 5
    dh = lam_q1.shape[-1]
    n_experts = re_w.shape[-1]
    alpha = (2 * depth) ** 0.25
    assert half % TN_PROJ == 0 and diff_subln_g.shape[-1] == 2 * dh and half == 2 * DIFF_HEADS * dh

    xt = x.reshape(t, d)
    rope = _rope_tables(positions, dh)
    kv = _inproj(mem.reshape(bsz * mem_len, d), mem_kv_w.astype(BF16))

    for l in range(depth):
        j = l // 2
        w_l = w_in[l].astype(BF16)
        if l % 2 == 0:
            qk_cols = 2 * half
            col = jnp.arange(qk_cols, dtype=jnp.int32)
            col = (col // dh) * dh + _rope_head_perm(dh)[col % dh]
            w_l = jnp.concatenate([jnp.take(w_l[:, :qk_cols], col, axis=1), w_l[:, qk_cols:]], axis=1)
            h = _inproj(xt, w_l, rope, n_rope_tiles=2 * half // TN_PROJ, n_scale_tiles=half // TN_PROJ,
                        scale=dh ** -0.5 * math.log2(math.e))
            lam_init = 0.8 - 0.6 * math.exp(-0.3 * l)
            lam = (jnp.exp(jnp.sum(lam_q1[j].astype(F32) * lam_k1[j].astype(F32)))
                   - jnp.exp(jnp.sum(lam_q2[j].astype(F32) * lam_k2[j].astype(F32))) + lam_init)
            o = _diff_attention(h, lam, diff_subln_g[j], bsz=bsz, seq=seq, out_scale=1.0 - lam_init)
            c = _conformer(h, conv_w[j], conv_b[j], conv_ln_g[j], conv_ln_b[j], seq=seq, col_a=3, col_g=4)
            parts = [o, c]
        else:
            h = _inproj(xt, w_l)
            parts = [_sgu_shortconv(h, sgu_ln_g[j], sgu_ln_b[j], sgu_w[j], sgu_b[j], sc_w[j], seq=seq)]
        n_route = N_GROUPS + n_experts
        wr = jnp.concatenate([rg_w[l], re_w[l], jnp.zeros((d, LANES - n_route), F32)], axis=1).astype(BF16)
        br = jnp.concatenate([rg_b[l], re_b[l], jnp.zeros((LANES - n_route,), F32)]).reshape(1, LANES).astype(F32)
        xt, route = _mix_cross_router(parts, w_out[l].astype(BF16), xt, ln_mix_g[l], ln_mix_b[l], kv,
                                      xq_w[l].astype(BF16), xo_w[l].astype(BF16), ln_mem_g[l], ln_mem_b[l], wr, br,
                                      seq=seq, mem_len=mem_len, alpha=alpha, per_group=n_experts // N_GROUPS)

        blk_e, n_used, tok, dst = _moe_dispatch(route, n_experts=n_experts, tb=TB_MOE)
        y = _moe_experts(xt, blk_e, n_used, tok, dst, e_w1, e_w3, e_w2, layer=l, n_out_rows=TOP_K * t + TB_MOE)
        xt = _combine_ln(xt, y, route, ln_ffn_g[l], ln_ffn_b[l], alpha=alpha)
    return xt.reshape(bsz, seq, d)
```
